```python
import jax
import jax.numpy as jnp
from jax import lax
import numpy as np

D_MODEL = 2048
BATCH = 1
SEQ = 8192
DEPTH = 2
DEC_BATCH = 128
DEC_SEQ = 1
PAST_LEN = 16384
PAGE_SIZE = 128

MIX_WIDTH = D_MODEL
MLA_WIDTH = MIX_WIDTH // 2
POOL_WIDTH = MIX_WIDTH // 4
CONV_WIDTH = MIX_WIDTH - MLA_WIDTH - POOL_WIDTH
NOPE_DIM = 128
ROPE_DIM = 64
V_DIM = 128
MLA_HEADS = MLA_WIDTH // V_DIM
Q_LORA = 768
KV_LORA = 256
ROPE_THETA = 10000.0
Q_BLOCK = 128
MLA_SCALE = (NOPE_DIM + ROPE_DIM) ** -0.5
POOL_WINDOWS = (2, 4, 8, 16)
POOL_GROUPS = len(POOL_WINDOWS)
POOL_GROUP_WIDTH = POOL_WIDTH // POOL_GROUPS
POOL_STATE = max(POOL_WINDOWS) - 1
CONV_K = 31
CONV_STATE = CONV_K - 1
IN_COLS = Q_LORA + KV_LORA + ROPE_DIM + POOL_WIDTH + 2 * CONV_WIDTH
MEM_TOKENS = 256
MEM_HEADS = 4
MEM_HEAD_DIM = 128
MEM_WIDTH = MEM_HEADS * MEM_HEAD_DIM
MEM_SCALE = MEM_HEAD_DIM ** -0.5
N_GROUPS = 4
EXPERTS_PER_GROUP = 8
N_EXPERTS = N_GROUPS * EXPERTS_PER_GROUP
TOP_K_IN_GROUP = 2
D_EXPERT = 512

EPS = 1e-6
NEG_INF = -1e30

kernel_name = 'hymba_mla_pool_conformer_hmoe_step'


def _rmsnorm(x, g):
    x32 = x.astype(jnp.float32)
    y = x32 * lax.rsqrt(jnp.mean(x32 * x32, axis=-1, keepdims=True) + EPS)
    return (y * g.astype(jnp.float32)).astype(x.dtype)


def _layernorm(x, g, b):
    x32 = x.astype(jnp.float32)
    mu = jnp.mean(x32, axis=-1, keepdims=True)
    var = jnp.mean(jnp.square(x32 - mu), axis=-1, keepdims=True)
    y = (x32 - mu) * lax.rsqrt(var + EPS)
    return (y * g.astype(jnp.float32) + b.astype(jnp.float32)).astype(x.dtype)


def _rope(x, pos):
    half = ROPE_DIM // 2
    inv = ROPE_THETA ** (-jnp.arange(half, dtype=jnp.float32) / half)
    ang = pos.astype(jnp.float32)[:, None] * inv[None, :]
    shape = (1, pos.shape[0]) + (1,) * (x.ndim - 3) + (half,)
    cos = jnp.cos(ang).reshape(shape).astype(x.dtype)
    sin = jnp.sin(ang).reshape(shape).astype(x.dtype)
    x1, x2 = x[..., :half], x[..., half:]
    return jnp.concatenate([x1 * cos - x2 * sin, x2 * cos + x1 * sin], axis=-1)


def _mixer_inputs(h, pos, g_mix, w_in, g_q_lat, w_uq, g_kv_lat):
    u = _rmsnorm(h, g_mix) @ w_in
    o1 = Q_LORA
    o2 = o1 + KV_LORA
    o3 = o2 + ROPE_DIM
    o4 = o3 + POOL_WIDTH
    c_q, c_kv, k_pe, x_pool, x_glu = u[..., :o1], u[..., o1:o2], u[..., o2:o3], u[..., o3:o4], u[..., o4:]
    q = jnp.einsum('btc,chd->bthd', _rmsnorm(c_q, g_q_lat), w_uq)
    q_nope = q[..., :NOPE_DIM]
    q_pe = _rope(q[..., NOPE_DIM:], pos)
    c_kv = _rmsnorm(c_kv, g_kv_lat)
    k_pe = _rope(k_pe, pos)
    x_conv = x_glu[..., :CONV_WIDTH] * jax.nn.sigmoid(x_glu[..., CONV_WIDTH:])
    return q_nope, q_pe, c_kv, k_pe, x_pool, x_conv


def _mla_prompt(q_nope, q_pe, c_kv, k_pe, pos, w_uk, w_uv):
    b, t = c_kv.shape[:2]
    k_nope = jnp.einsum('btc,chn->bthn', c_kv, w_uk)
    v = jnp.einsum('btc,chv->bthv', c_kv, w_uv)
    nb = t // Q_BLOCK

    def blocks(a):
        return jnp.moveaxis(a.reshape((b, nb, Q_BLOCK) + a.shape[2:]), 1, 0)

    def one_block(args):
        qn, qp, qpos = args
        s = jnp.einsum('bqhn,bkhn->bhqk', qn, k_nope) + jnp.einsum('bqhr,bkr->bhqk', qp, k_pe)
        s = s.astype(jnp.float32) * MLA_SCALE
        s = jnp.where(pos[None, :] <= qpos[:, None], s, NEG_INF)
        p = jax.nn.softmax(s, axis=-1).astype(v.dtype)
        return jnp.einsum('bhqk,bkhv->bqhv', p, v)

    o = lax.map(one_block, (blocks(q_nope), blocks(q_pe), pos.reshape(nb, Q_BLOCK)))
    return jnp.moveaxis(o, 0, 1).reshape(b, t, MLA_HEADS * V_DIM)


def _mla_sample(q_nope, q_pe, c_kv_new, k_pe_new, ckv_past, kpe_past, w_uk, w_uv):
    b, s_new = q_nope.shape[:2]
    p_len = ckv_past.shape[1]
    q_lat = jnp.einsum('bshn,chn->bshc', q_nope, w_uk)
    s_past = jnp.einsum('bshc,btc->bhst', q_lat, ckv_past) + jnp.einsum('bshr,btr->bhst', q_pe, kpe_past)
    s_self = jnp.einsum('bshc,btc->bhst', q_lat, c_kv_new) + jnp.einsum('bshr,btr->bhst', q_pe, k_pe_new)
    causal = jnp.arange(s_new)[None, :] <= jnp.arange(s_new)[:, None]
    s_self = jnp.where(causal, s_self.astype(jnp.float32) * MLA_SCALE, NEG_INF)
    s = jnp.concatenate([s_past.astype(jnp.float32) * MLA_SCALE, s_self], axis=-1)
    p = jax.nn.softmax(s, axis=-1).astype(c_kv_new.dtype)
    o_lat = jnp.einsum('bhst,btc->bshc', p[..., :p_len], ckv_past) + jnp.einsum('bhst,btc->bshc', p[..., p_len:], c_kv_new)
    return jnp.einsum('bshc,chv->bshv', o_lat, w_uv).reshape(b, s_new, MLA_HEADS * V_DIM)


def _pool_mix(ext, pos, w_pool, pool_scale):
    b = ext.shape[0]
    t = pos.shape[0]
    e32 = ext.astype(jnp.float32)
    cs = jnp.cumsum(e32, axis=1)
    cs = jnp.concatenate([jnp.zeros_like(cs[:, :1]), cs], axis=1)
    end = cs[:, POOL_STATE + 1:]
    x_new = e32[:, POOL_STATE:]
    outs = []
    for gi, w in enumerate(POOL_WINDOWS):
        sl = slice(gi * POOL_GROUP_WIDTH, (gi + 1) * POOL_GROUP_WIDTH)
        start = cs[:, POOL_STATE + 1 - w: POOL_STATE + 1 - w + t, sl]
        count = jnp.minimum(pos + 1, w).astype(jnp.float32)[None, :, None]
        outs.append((end[..., sl] - start) / count - x_new[..., sl])
    pooled = jnp.concatenate(outs, axis=-1).astype(ext.dtype).reshape(b, t, POOL_GROUPS, POOL_GROUP_WIDTH)
    y = jnp.einsum('btgc,gcd->btgd', pooled, w_pool).reshape(b, t, POOL_WIDTH)
    return y * pool_scale


def _conv_mix(ext, w_dw, b_dw, g_ln, b_ln, w_pw):
    y = lax.conv_general_dilated(ext, w_dw[:, None, :], window_strides=(1,), padding='VALID',
                                 dimension_numbers=('NWC', 'WIO', 'NWC'), feature_group_count=CONV_WIDTH)
    y = _layernorm(y + b_dw, g_ln, b_ln)
    return jax.nn.silu(y) @ w_pw


def _mem_kv(mem, g, w_mk, w_mv):
    m = _rmsnorm(mem, g)
    b, n = mem.shape[:2]
    return (m @ w_mk).reshape(b, n, MEM_HEADS, MEM_HEAD_DIM), (m @ w_mv).reshape(b, n, MEM_HEADS, MEM_HEAD_DIM)


def _cross_attn(h, g, w_mq, k, v, w_mo):
    b, t = h.shape[:2]
    q = (_rmsnorm(h, g) @ w_mq).reshape(b, t, MEM_HEADS, MEM_HEAD_DIM)
    s = jnp.einsum('bthd,bmhd->bhtm', q, k).astype(jnp.float32) * MEM_SCALE
    p = jax.nn.softmax(s, axis=-1).astype(v.dtype)
    o = jnp.einsum('bhtm,bmhd->bthd', p, v).reshape(b, t, MEM_WIDTH)
    return o @ w_mo


def _hier_moe(h, g, w_rg, b_rg, w_re, b_re, w_gate, w_up, w_down):
    x = _rmsnorm(h, g)
    b, t = x.shape[:2]
    lg = (x @ w_rg + b_rg).astype(jnp.float32)
    pg = jax.nn.softmax(lg, axis=-1)
    g_sel = jnp.argmax(lg, axis=-1)
    p_sel = jnp.max(pg, axis=-1, keepdims=True)
    le = (x @ w_re + b_re).astype(jnp.float32).reshape(b, t, N_GROUPS, EXPERTS_PER_GROUP)
    le_sel = jnp.einsum('btge,btg->bte', le, jax.nn.one_hot(g_sel, N_GROUPS, dtype=jnp.float32))
    top_v, top_i = lax.top_k(le_sel, TOP_K_IN_GROUP)
    w_sel = jax.nn.softmax(top_v, axis=-1) * p_sel
    expert_id = g_sel[..., None] * EXPERTS_PER_GROUP + top_i
    gates = jnp.sum(jax.nn.one_hot(expert_id, N_EXPERTS, dtype=jnp.float32) * w_sel[..., None], axis=-2)
    hg = jnp.einsum('btd,edf->btef', x, w_gate)
    hu = jnp.einsum('btd,edf->btef', x, w_up)
    act = jax.nn.silu(hg) * hu * gates[..., None].astype(x.dtype)
    return jnp.einsum('btef,efd->btd', act, w_down)


def setup_inputs(seed: int = 0) -> dict:
    key = jax.random.key(seed)
    keys = iter(jax.random.split(key, 64))
    f32 = jnp.float32

    def normal(shape, scale=1.0):
        return scale * jax.random.normal(next(keys), shape, f32)

    def dense(shape, fan_in):
        return jax.random.normal(next(keys), shape, f32) * (fan_in ** -0.5)

    def gain(shape):
        return 1.0 + 0.02 * jax.random.normal(next(keys), shape, f32)

    n_pages = PAST_LEN // PAGE_SIZE
    n_used = DEC_BATCH * n_pages
    n_phys = n_used + (n_used + 3) // 4
    perm = jax.random.permutation(next(keys), n_phys)
    page_table = perm[:n_used].reshape(DEC_BATCH, n_pages).astype(jnp.int32)
    L = DEPTH
    return {
        'x_prompt': normal((BATCH, SEQ, D_MODEL)),
        'x_sample': normal((DEC_BATCH, DEC_SEQ, D_MODEL)),
        'mem_prompt': normal((BATCH, MEM_TOKENS, D_MODEL)),
        'cache_ckv': normal((L, n_phys, PAGE_SIZE, KV_LORA)),
        'cache_kpe': normal((L, n_phys, PAGE_SIZE, ROPE_DIM)),
        'page_table': page_table,
        'state_pool': normal((L, DEC_BATCH, POOL_STATE, POOL_WIDTH)),
        'state_conv': normal((L, DEC_BATCH, CONV_STATE, CONV_WIDTH), 0.5),
        'cache_mem_k': normal((L, DEC_BATCH, MEM_TOKENS, MEM_HEADS, MEM_HEAD_DIM)),
        'cache_mem_v': normal((L, DEC_BATCH, MEM_TOKENS, MEM_HEADS, MEM_HEAD_DIM)),
        'g_mix': gain((L, D_MODEL)),
        'w_in': dense((L, D_MODEL, IN_COLS), D_MODEL),
        'g_q_lat': gain((L, Q_LORA)),
        'w_uq': dense((L, Q_LORA, MLA_HEADS, NOPE_DIM + ROPE_DIM), Q_LORA),
        'g_kv_lat': gain((L, KV_LORA)),
        'w_uk': dense((L, KV_LORA, MLA_HEADS, NOPE_DIM), KV_LORA),
        'w_uv': dense((L, KV_LORA, MLA_HEADS, V_DIM), KV_LORA),
        'w_pool': dense((L, POOL_GROUPS, POOL_GROUP_WIDTH, POOL_GROUP_WIDTH), POOL_GROUP_WIDTH),
        'pool_scale': gain((L, POOL_WIDTH)),
        'w_conv_dw': dense((L, CONV_K, CONV_WIDTH), CONV_K),
        'b_conv_dw': normal((L, CONV_WIDTH), 0.02),
        'g_conv_ln': gain((L, CONV_WIDTH)),
        'b_conv_ln': normal((L, CONV_WIDTH), 0.02),
        'w_conv_pw': dense((L, CONV_WIDTH, CONV_WIDTH), CONV_WIDTH),
        'w_out': dense((L, MIX_WIDTH, D_MODEL), MIX_WIDTH),
        'g_mem_x': gain((L, D_MODEL)),
        'g_mem_tok': gain((L, D_MODEL)),
        'w_mq': dense((L, D_MODEL, MEM_WIDTH), D_MODEL),
        'w_mk': dense((L, D_MODEL, MEM_WIDTH), D_MODEL),
        'w_mv': dense((L, D_MODEL, MEM_WIDTH), D_MODEL),
        'w_mo': dense((L, MEM_WIDTH, D_MODEL), MEM_WIDTH),
        'g_ffn': gain((L, D_MODEL)),
        'w_route_group': dense((L, D_MODEL, N_GROUPS), D_MODEL),
        'b_route_group': normal((L, N_GROUPS), 0.01),
        'w_route_expert': dense((L, D_MODEL, N_EXPERTS), D_MODEL),
        'b_route_expert': normal((L, N_EXPERTS), 0.01),
        'w_gate': dense((L, N_EXPERTS, D_MODEL, D_EXPERT), D_MODEL),
        'w_up': dense((L, N_EXPERTS, D_MODEL, D_EXPERT), D_MODEL),
        'w_down': dense((L, N_EXPERTS, D_EXPERT, D_MODEL), D_EXPERT),
        'g_final': gain((D_MODEL,)),
    }


def reference(x_prompt, x_sample, mem_prompt, cache_ckv, cache_kpe, page_table, state_pool, state_conv,
              cache_mem_k, cache_mem_v, g_mix, w_in, g_q_lat, w_uq, g_kv_lat, w_uk, w_uv, w_pool, pool_scale,
              w_conv_dw, b_conv_dw, g_conv_ln, b_conv_ln, w_conv_pw, w_out, g_mem_x, g_mem_tok, w_mq, w_mk,
              w_mv, w_mo, g_ffn, w_route_group, b_route_group, w_route_expert, b_route_expert, w_gate, w_up,
              w_down, g_final):
    n_prompt, t_prompt = x_prompt.shape[:2]
    n_dec, s_dec = x_sample.shape[:2]
    n_pages = page_table.shape[1]
    pos_p = jnp.arange(t_prompt, dtype=jnp.int32)
    pos_s = PAST_LEN + jnp.arange(s_dec, dtype=jnp.int32)
    hp, hs = x_prompt, x_sample
    ckv_p, kpe_p, pool_p, conv_p, mk_p, mv_p = [], [], [], [], [], []
    ckv_s, kpe_s, pool_s, conv_s = [], [], [], []
    for l in range(DEPTH):
        mix_w = (g_mix[l], w_in[l], g_q_lat[l], w_uq[l], g_kv_lat[l])
        conv_w = (w_conv_dw[l], b_conv_dw[l], g_conv_ln[l], b_conv_ln[l], w_conv_pw[l])
        moe_w = (g_ffn[l], w_route_group[l], b_route_group[l], w_route_expert[l], b_route_expert[l],
                 w_gate[l], w_up[l], w_down[l])

        qn, qp, ckv, kpe, xpool, xconv = _mixer_inputs(hp, pos_p, *mix_w)
        o_mla = _mla_prompt(qn, qp, ckv, kpe, pos_p, w_uk[l], w_uv[l])
        ext_pool = jnp.concatenate([jnp.zeros((n_prompt, POOL_STATE, POOL_WIDTH), xpool.dtype), xpool], axis=1)
        o_pool = _pool_mix(ext_pool, pos_p, w_pool[l], pool_scale[l])
        ext_conv = jnp.concatenate([jnp.zeros((n_prompt, CONV_STATE, CONV_WIDTH), xconv.dtype), xconv], axis=1)
        o_conv = _conv_mix(ext_conv, *conv_w)
        hp = hp + jnp.concatenate([o_mla, o_pool, o_conv], axis=-1) @ w_out[l]
        ckv_p.append(ckv)
        kpe_p.append(kpe)
        pool_p.append(ext_pool[:, -POOL_STATE:])
        conv_p.append(ext_conv[:, -CONV_STATE:])
        mk, mv = _mem_kv(mem_prompt, g_mem_tok[l], w_mk[l], w_mv[l])
        mk_p.append(mk)
        mv_p.append(mv)
        hp = hp + _cross_attn(hp, g_mem_x[l], w_mq[l], mk, mv, w_mo[l])
        hp = hp + _hier_moe(hp, *moe_w)

        qn, qp, ckv, kpe, xpool, xconv = _mixer_inputs(hs, pos_s, *mix_w)
        ckv_past = cache_ckv[l][page_table].reshape(n_dec, n_pages * PAGE_SIZE, KV_LORA)
        kpe_past = cache_kpe[l][page_table].reshape(n_dec, n_pages * PAGE_SIZE, ROPE_DIM)
        o_mla = _mla_sample(qn, qp, ckv, kpe, ckv_past, kpe_past, w_uk[l], w_uv[l])
        ext_pool = jnp.concatenate([state_pool[l].astype(xpool.dtype), xpool], axis=1)
        o_pool = _pool_mix(ext_pool, pos_s, w_pool[l], pool_scale[l])
        ext_conv = jnp.concatenate([state_conv[l].astype(xconv.dtype), xconv], axis=1)
        o_conv = _conv_mix(ext_conv, *conv_w)
        hs = hs + jnp.concatenate([o_mla, o_pool, o_conv], axis=-1) @ w_out[l]
        ckv_s.append(ckv)
        kpe_s.append(kpe)
        pool_s.append(ext_pool[:, -POOL_STATE:])
        conv_s.append(ext_conv[:, -CONV_STATE:])
        hs = hs + _cross_attn(hs, g_mem_x[l], w_mq[l], cache_mem_k[l], cache_mem_v[l], w_mo[l])
        hs = hs + _hier_moe(hs, *moe_w)

    y_prompt = _rmsnorm(hp, g_final)
    y_sample = _rmsnorm(hs, g_final)
    return (y_prompt, y_sample,
            jnp.stack(ckv_p), jnp.stack(kpe_p), jnp.stack(pool_p), jnp.stack(conv_p),
            jnp.stack(mk_p), jnp.stack(mv_p),
            jnp.stack(ckv_s), jnp.stack(kpe_s), jnp.stack(pool_s), jnp.stack(conv_s))
```

```python
import functools

import jax
import jax.numpy as jnp
from jax import lax
from jax.experimental import pallas as pl
from jax.experimental.pallas import tpu as pltpu

F32 = jnp.float32
BF16 = jnp.bfloat16

D_MODEL = 2048
DEPTH = 2
PAST_LEN = 16384
PAGE_SIZE = 128
NOPE_DIM = 128
ROPE_DIM = 64
V_DIM = 128
MLA_HEADS = 8
Q_LORA = 768
KV_LORA = 256
ROPE_THETA = 10000.0
MLA_SCALE = (NOPE_DIM + ROPE_DIM) ** -0.5
POOL_WIDTH = 512
CONV_WIDTH = 512
POOL_WINDOWS = (2, 4, 8, 16)
POOL_GROUP_WIDTH = 128
POOL_STATE = 15
CONV_K = 31
CONV_STATE = 30
MEM_TOKENS = 256
MEM_HEADS = 4
MEM_HEAD_DIM = 128
MEM_WIDTH = 512
MEM_SCALE = MEM_HEAD_DIM ** -0.5
N_GROUPS = 4
EXPERTS_PER_GROUP = 8
N_EXPERTS = 32
D_EXPERT = 512
EPS = 1e-6
NEG_INF = -1e30

_O_CQ = Q_LORA
_O_CKV = _O_CQ + KV_LORA
_O_POOL = _O_CKV + POOL_WIDTH
_O_GLU = _O_POOL + 2 * CONV_WIDTH
IN_COLS_R = _O_GLU + 2 * ROPE_DIM
Q_CHUNK = NOPE_DIM + 2 * ROPE_DIM

VMEM_LIMIT_V7X = 56 * 1024 * 1024
ROUTE_LANES = 128
PAGES_PER_STEP = 16
MOE_TILE = 256


def _params(sem):
    return pltpu.CompilerParams(dimension_semantics=sem, vmem_limit_bytes=VMEM_LIMIT_V7X)


def _const(shape):
    return pl.BlockSpec(shape, lambda *_: (0,) * len(shape), pipeline_mode=pl.Buffered(1))


def _rms(x, g):
    ms = jnp.mean(x * x, axis=-1, keepdims=True)
    return x * lax.rsqrt(ms + EPS) * g


def _precision(a):
    return lax.Precision.HIGHEST if a.dtype == F32 else None


def _dot(a, b):
    return jnp.dot(a, b, preferred_element_type=F32, precision=_precision(a))


def _dot_nt(a, b):
    return lax.dot_general(a, b, (((1,), (1,)), ((), ())), preferred_element_type=F32, precision=_precision(a))


def _mixer_common(h_ref, tab_ref, gmix_ref, win_ref, gq_ref, wq_ref, gkv_ref,
                  ckv_ref, kpe_ref, xpool_ref, xconv_ref):
    xn = _rms(h_ref[...], gmix_ref[...]).astype(BF16)
    u = _dot(xn, win_ref[...])
    cq = _rms(u[:, :_O_CQ], gq_ref[...]).astype(BF16)
    ckv = _rms(u[:, _O_CQ:_O_CKV], gkv_ref[...])
    ckv_ref[...] = ckv
    xpool_ref[...] = u[:, _O_CKV:_O_POOL]
    xconv_ref[...] = u[:, _O_POOL:_O_POOL + CONV_WIDTH] * jax.nn.sigmoid(u[:, _O_POOL + CONV_WIDTH:_O_GLU])
    tab = tab_ref[...]
    kp = u[:, _O_GLU:] * tab
    kp2 = kp + pltpu.roll(kp, ROPE_DIM, 1)
    kpe_ref[...] = kp2[:, :ROPE_DIM]
    qraw = _dot(cq, wq_ref[...])
    return ckv, kp2, qraw, tab


def _mixer_prompt_kernel(h_ref, tab_ref, gmix_ref, win_ref, gq_ref, wq_ref, gkv_ref, wuk_ref, wuv_ref,
                         q_ref, k_ref, v_ref, ckv_ref, kpe_ref, xpool_ref, xconv_ref):
    ckv, kp2, qraw, tab = _mixer_common(h_ref, tab_ref, gmix_ref, win_ref, gq_ref, wq_ref, gkv_ref,
                                        ckv_ref, kpe_ref, xpool_ref, xconv_ref)
    ckv_b = ckv.astype(BF16)
    kn = _dot(ckv_b, wuk_ref[...])
    vv = _dot(ckv_b, wuv_ref[...])
    kp2b = kp2.astype(BF16)
    for hh in range(MLA_HEADS):
        c0 = Q_CHUNK * hh
        q_ref[hh, :, :NOPE_DIM] = qraw[:, c0:c0 + NOPE_DIM].astype(BF16)
        q_ref[hh, :, NOPE_DIM:] = (qraw[:, c0 + NOPE_DIM:c0 + Q_CHUNK] * tab).astype(BF16)
        k_ref[hh, :, :NOPE_DIM] = kn[:, NOPE_DIM * hh:NOPE_DIM * (hh + 1)].astype(BF16)
        k_ref[hh, :, NOPE_DIM:] = kp2b
        v_ref[hh] = vv[:, V_DIM * hh:V_DIM * (hh + 1)].astype(BF16)


def _mixer_sample_kernel(h_ref, tab_ref, gmix_ref, win_ref, gq_ref, wq_ref, gkv_ref, wukt_ref,
                         qlat_ref, qpe_ref, ckv_ref, kpe_ref, xpool_ref, xconv_ref):
    _, _, qraw, tab = _mixer_common(h_ref, tab_ref, gmix_ref, win_ref, gq_ref, wq_ref, gkv_ref,
                                    ckv_ref, kpe_ref, xpool_ref, xconv_ref)
    for hh in range(MLA_HEADS):
        c0 = Q_CHUNK * hh
        qn = qraw[:, c0:c0 + NOPE_DIM].astype(BF16)
        qlat_ref[hh] = _dot(qn, wukt_ref[hh]).astype(BF16)
        qp = qraw[:, c0 + NOPE_DIM:c0 + Q_CHUNK] * tab
        qp2 = qp + pltpu.roll(qp, ROPE_DIM, 1)
        qpe_ref[hh] = qp2[:, :ROPE_DIM].astype(BF16)


def _mixer_in(h, tab, lw, *, prompt):
    t = h.shape[0]
    tm = 256 if prompt else t
    row = lambda w: pl.BlockSpec((tm, w), lambda i: (i, 0))
    head = lambda w: pl.BlockSpec((MLA_HEADS, tm, w), lambda i: (0, i, 0))
    in_specs = [row(D_MODEL), row(2 * ROPE_DIM), _const((1, D_MODEL)), _const((D_MODEL, IN_COLS_R)),
                _const((1, Q_LORA)), _const((Q_LORA, MLA_HEADS * Q_CHUNK)), _const((1, KV_LORA))]
    args = [h, tab, lw['g_mix'], lw['w_in'], lw['g_q_lat'], lw['w_q'], lw['g_kv_lat']]
    tail_specs = [row(KV_LORA), row(ROPE_DIM), row(POOL_WIDTH), row(CONV_WIDTH)]
    tail_shapes = [jax.ShapeDtypeStruct((t, KV_LORA), F32), jax.ShapeDtypeStruct((t, ROPE_DIM), F32),
                   jax.ShapeDtypeStruct((t, POOL_WIDTH), F32), jax.ShapeDtypeStruct((t, CONV_WIDTH), F32)]
    if prompt:
        kern = _mixer_prompt_kernel
        in_specs += [_const((KV_LORA, MLA_HEADS * NOPE_DIM)), _const((KV_LORA, MLA_HEADS * V_DIM))]
        args += [lw['w_uk'], lw['w_uv']]
        out_specs = [head(2 * NOPE_DIM), head(2 * NOPE_DIM), head(V_DIM)] + tail_specs
        out_shape = [jax.ShapeDtypeStruct((MLA_HEADS, t, 2 * NOPE_DIM), BF16),
                     jax.ShapeDtypeStruct((MLA_HEADS, t, 2 * NOPE_DIM), BF16),
                     jax.ShapeDtypeStruct((MLA_HEADS, t, V_DIM), BF16)] + tail_shapes
    else:
        kern = _mixer_sample_kernel
        in_specs += [_const((MLA_HEADS, NOPE_DIM, KV_LORA))]
        args += [lw['w_uk_t']]
        out_specs = [head(KV_LORA), head(ROPE_DIM)] + tail_specs
        out_shape = [jax.ShapeDtypeStruct((MLA_HEADS, t, KV_LORA), BF16),
                     jax.ShapeDtypeStruct((MLA_HEADS, t, ROPE_DIM), BF16)] + tail_shapes
    return pl.pallas_call(kern, grid=(t // tm,), in_specs=in_specs, out_specs=out_specs,
                          out_shape=out_shape, compiler_params=_params(("parallel",)))(*args)


def _flash_kernel(q_ref, k_ref, v_ref, o_ref, m_scr, l_scr, acc_scr, *, blk):
    qi = pl.program_id(1)
    q = q_ref[0]
    m_scr[...] = jnp.full_like(m_scr, -jnp.inf)
    l_scr[...] = jnp.zeros_like(l_scr)
    acc_scr[...] = jnp.zeros_like(acc_scr)

    def step(kv, diagonal):
        start = pl.multiple_of(kv * blk, blk)
        k = k_ref[0, pl.ds(start, blk), :]
        v = v_ref[0, pl.ds(start, blk), :]
        s = _dot_nt(q, k)
        if diagonal:
            row = lax.broadcasted_iota(jnp.int32, (blk, blk), 0)
            col = lax.broadcasted_iota(jnp.int32, (blk, blk), 1)
            s = jnp.where(col <= row, s, NEG_INF)
        m_prev = m_scr[...]
        m_new = jnp.maximum(m_prev, jnp.max(s, axis=1, keepdims=True))
        p = jnp.exp(s - m_new)
        alpha = jnp.exp(m_prev - m_new)
        l_scr[...] = alpha * l_scr[...] + jnp.sum(p, axis=1, keepdims=True)
        acc_scr[...] = alpha * acc_scr[...] + _dot(p.astype(BF16), v)
        m_scr[...] = m_new

    def body(kv, carry):
        step(kv, False)
        return carry

    lax.fori_loop(0, qi, body, 0)
    step(qi, True)
    o_ref[...] = (acc_scr[...] / l_scr[...]).astype(o_ref.dtype)


def _flash_attention(q, k, v):
    nh, t, dk = q.shape
    blk = 512
    return pl.pallas_call(
        functools.partial(_flash_kernel, blk=blk),
        grid=(nh, t // blk),
        in_specs=[pl.BlockSpec((1, blk, dk), lambda h, i: (h, i, 0)),
                  pl.BlockSpec((1, t, dk), lambda h, i: (h, 0, 0)),
                  pl.BlockSpec((1, t, V_DIM), lambda h, i: (h, 0, 0))],
        out_specs=pl.BlockSpec((blk, V_DIM), lambda h, i: (i, h)),
        out_shape=jax.ShapeDtypeStruct((t, nh * V_DIM), F32),
        scratch_shapes=[pltpu.VMEM((blk, 1), F32), pltpu.VMEM((blk, 1), F32), pltpu.VMEM((blk, V_DIM), F32)],
        compiler_params=_params(("parallel", "parallel")))(q, k, v)


def _paged_kernel(pt_ref, qlat_ref, qpe_ref, ckvn_ref, kpen_ref, *rest, n_steps):
    del pt_ref
    ckv_refs = rest[:PAGES_PER_STEP]
    kpe_refs = rest[PAGES_PER_STEP:2 * PAGES_PER_STEP]
    o_ref, m_scr, l_scr, acc_scr = rest[2 * PAGES_PER_STEP:]
    j = pl.program_id(1)
    ql = qlat_ref[0]
    qp = qpe_ref[0]

    @pl.when(j == 0)
    def _():
        cn = ckvn_ref[0]
        kn = kpen_ref[0]
        s_self = (jnp.sum(ql.astype(F32) * cn, axis=1, keepdims=True)
                  + jnp.sum(qp.astype(F32) * kn, axis=1, keepdims=True))
        m_scr[...] = s_self
        l_scr[...] = jnp.ones_like(l_scr)
        acc_scr[...] = jnp.broadcast_to(cn, acc_scr.shape)

    kcs, ss = [], []
    for i in range(PAGES_PER_STEP):
        kc = ckv_refs[i][...].astype(BF16)
        kp = kpe_refs[i][...].astype(BF16)
        kcs.append(kc)
        ss.append(_dot_nt(ql, kc) + _dot_nt(qp, kp))
    s = jnp.concatenate(ss, axis=1)
    m_prev = m_scr[...]
    m_new = jnp.maximum(m_prev, jnp.max(s, axis=1, keepdims=True))
    p = jnp.exp(s - m_new)
    alpha = jnp.exp(m_prev - m_new)
    l_scr[...] = alpha * l_scr[...] + jnp.sum(p, axis=1, keepdims=True)
    pb = p.astype(BF16)
    pv = _dot(pb[:, :PAGE_SIZE], kcs[0])
    for i in range(1, PAGES_PER_STEP):
        pv = pv + _dot(pb[:, i * PAGE_SIZE:(i + 1) * PAGE_SIZE], kcs[i])
    acc_scr[...] = alpha * acc_scr[...] + pv
    m_scr[...] = m_new

    @pl.when(j == n_steps - 1)
    def _():
        o_ref[0] = acc_scr[...] / l_scr[...]


def _paged_attention(layer, page_table, qlat, qpe, ckv_new, kpe_new, cache_ckv, cache_kpe):
    s, n_pages = page_table.shape
    n_steps = n_pages // PAGES_PER_STEP

    def page_spec(width, i):
        return pl.BlockSpec((None, None, PAGE_SIZE, width),
                            lambda b, j, pt: (layer, pt[b, j * PAGES_PER_STEP + i], 0, 0))

    per_seq = lambda r, w: pl.BlockSpec((1, r, w), lambda b, j, pt: (b, 0, 0))
    in_specs = ([per_seq(MLA_HEADS, KV_LORA), per_seq(MLA_HEADS, ROPE_DIM), per_seq(1, KV_LORA), per_seq(1, ROPE_DIM)]
                + [page_spec(KV_LORA, i) for i in range(PAGES_PER_STEP)]
                + [page_spec(ROPE_DIM, i) for i in range(PAGES_PER_STEP)])
    grid_spec = pltpu.PrefetchScalarGridSpec(
        num_scalar_prefetch=1, grid=(s, n_steps), in_specs=in_specs,
        out_specs=per_seq(MLA_HEADS, KV_LORA),
        scratch_shapes=[pltpu.VMEM((MLA_HEADS, 1), F32), pltpu.VMEM((MLA_HEADS, 1), F32),
                        pltpu.VMEM((MLA_HEADS, KV_LORA), F32)])
    return pl.pallas_call(
        functools.partial(_paged_kernel, n_steps=n_steps), grid_spec=grid_spec,
        out_shape=jax.ShapeDtypeStruct((s, MLA_HEADS, KV_LORA), F32),
        compiler_params=_params(("parallel", "arbitrary")))(
            page_table, qlat, qpe, ckv_new, kpe_new,
            *([cache_ckv] * PAGES_PER_STEP), *([cache_kpe] * PAGES_PER_STEP))


def _value_up_kernel(olat_ref, wuv_ref, o_ref):
    for hh in range(MLA_HEADS):
        o_ref[:, V_DIM * hh:V_DIM * (hh + 1)] = _dot(olat_ref[hh].astype(BF16), wuv_ref[hh]).astype(o_ref.dtype)


def _value_up(olat_h, w_uv_h):
    s = olat_h.shape[1]
    return pl.pallas_call(_value_up_kernel, out_shape=jax.ShapeDtypeStruct((s, MLA_HEADS * V_DIM), BF16))(olat_h, w_uv_h)


def _ln_silu(y, g, b):
    mu = jnp.mean(y, axis=-1, keepdims=True)
    d = y - mu
    var = jnp.mean(d * d, axis=-1, keepdims=True)
    z = d * lax.rsqrt(var + EPS) * g + b
    return z * jax.nn.sigmoid(z)


_POOL_PAD = 16
_CONV_PAD = 32


def _poolconv_prompt_kernel(xp_ref, xc_ref, wdw_ref, bdw_ref, gln_ref, bln_ref, pooled_ref, cact_ref,
                            extp, extc, *, tm, pos0):
    i = pl.program_id(0)

    @pl.when(i == 0)
    def _():
        extp[0:_POOL_PAD, :] = jnp.zeros((_POOL_PAD, POOL_WIDTH), F32)
        extc[0:_CONV_PAD, :] = jnp.zeros((_CONV_PAD, CONV_WIDTH), F32)

    extp[_POOL_PAD:, :] = xp_ref[...]
    extc[_CONV_PAD:, :] = xc_ref[...]
    pos = pos0 + i * tm + lax.broadcasted_iota(jnp.int32, (tm, 1), 0)
    for gi, w in enumerate(POOL_WINDOWS):
        c0 = gi * POOL_GROUP_WIDTH
        x_new = extp[_POOL_PAD:, c0:c0 + POOL_GROUP_WIDTH]
        acc = x_new
        for k in range(1, w):
            acc = acc + extp[_POOL_PAD - k:_POOL_PAD - k + tm, c0:c0 + POOL_GROUP_WIDTH]
        cnt = jnp.minimum(pos + 1, w).astype(F32)
        pooled_ref[:, c0:c0 + POOL_GROUP_WIDTH] = (acc / cnt - x_new).astype(pooled_ref.dtype)
    base = _CONV_PAD - CONV_STATE
    acc = extc[base:base + tm, :] * wdw_ref[0:1, :]
    for k in range(1, CONV_K):
        acc = acc + extc[base + k:base + k + tm, :] * wdw_ref[k:k + 1, :]
    cact_ref[...] = _ln_silu(acc + bdw_ref[...], gln_ref[...], bln_ref[...]).astype(cact_ref.dtype)
    extp[0:_POOL_PAD, :] = extp[tm:tm + _POOL_PAD, :]
    extc[0:_CONV_PAD, :] = extc[tm:tm + _CONV_PAD, :]


def _poolconv_prompt(xpool, xconv, lw, *, pos0=0, out_dtype=BF16):
    t = xpool.shape[0]
    tm = min(t, 512)
    row = pl.BlockSpec((tm, POOL_WIDTH), lambda i: (i, 0))
    vec = _const((1, CONV_WIDTH))
    return pl.pallas_call(
        functools.partial(_poolconv_prompt_kernel, tm=tm, pos0=pos0), grid=(t // tm,),
        in_specs=[row, row, _const((CONV_K, CONV_WIDTH)), vec, vec, vec],
        out_specs=[row, row],
        out_shape=[jax.ShapeDtypeStruct((t, POOL_WIDTH), out_dtype), jax.ShapeDtypeStruct((t, CONV_WIDTH), out_dtype)],
        scratch_shapes=[pltpu.VMEM((tm + _POOL_PAD, POOL_WIDTH), F32), pltpu.VMEM((tm + _CONV_PAD, CONV_WIDTH), F32)],
        compiler_params=_params(("arbitrary",)))(
            xpool, xconv, lw['w_conv_dw'], lw['b_conv_dw'], lw['g_conv_ln'], lw['b_conv_ln'])


def _poolconv_sample_kernel(sp_ref, sc_ref, xp_ref, xc_ref, wdw_ref, bdw_ref, gln_ref, bln_ref,
                            pooled_ref, cact_ref):
    for gi, w in enumerate(POOL_WINDOWS):
        c0 = gi * POOL_GROUP_WIDTH
        x_new = xp_ref[:, c0:c0 + POOL_GROUP_WIDTH]
        acc = x_new
        for k in range(1, w):
            acc = acc + sp_ref[POOL_STATE - k, :, c0:c0 + POOL_GROUP_WIDTH]
        cnt = float(min(PAST_LEN + 1, w))
        pooled_ref[:, c0:c0 + POOL_GROUP_WIDTH] = (acc / cnt - x_new).astype(pooled_ref.dtype)
    acc = xc_ref[...] * wdw_ref[CONV_K - 1:CONV_K, :]
    for k in range(CONV_STATE):
        acc = acc + sc_ref[k] * wdw_ref[k:k + 1, :]
    cact_ref[...] = _ln_silu(acc + bdw_ref[...], gln_ref[...], bln_ref[...]).astype(cact_ref.dtype)


def _poolconv_sample(state_pool_t, state_conv_t, xpool, xconv, lw):
    s = xpool.shape[0]
    return pl.pallas_call(
        _poolconv_sample_kernel,
        out_shape=[jax.ShapeDtypeStruct((s, POOL_WIDTH), BF16), jax.ShapeDtypeStruct((s, CONV_WIDTH), BF16)])(
            state_pool_t, state_conv_t, xpool, xconv,
            lw['w_conv_dw'], lw['b_conv_dw'], lw['g_conv_ln'], lw['b_conv_ln'])


def _mix_out_kernel(h_ref, omla_ref, pooled_ref, cact_ref, wpool_ref, pscale_ref, wpw_ref, wout_ref, o_ref):
    cdt = wout_ref.dtype
    acc = h_ref[...] + _dot(omla_ref[...].astype(cdt), wout_ref[0:1024, :])
    pooled = pooled_ref[...]
    for gi in range(len(POOL_WINDOWS)):
        c0 = gi * POOL_GROUP_WIDTH
        og = _dot(pooled[:, c0:c0 + POOL_GROUP_WIDTH], wpool_ref[gi]) * pscale_ref[:, c0:c0 + POOL_GROUP_WIDTH]
        acc = acc + _dot(og.astype(cdt), wout_ref[1024 + c0:1024 + c0 + POOL_GROUP_WIDTH, :])
    oc = _dot(cact_ref[...], wpw_ref[...]).astype(cdt)
    o_ref[...] = acc + _dot(oc, wout_ref[1536:2048, :])


def _mix_out(h, omla, pooled, cact, lw):
    t = h.shape[0]
    tm = min(t, 512)
    row = lambda w: pl.BlockSpec((tm, w), lambda i: (i, 0))
    return pl.pallas_call(
        _mix_out_kernel, grid=(t // tm,),
        in_specs=[row(D_MODEL), row(1024), row(POOL_WIDTH), row(CONV_WIDTH),
                  _const((len(POOL_WINDOWS), POOL_GROUP_WIDTH, POOL_GROUP_WIDTH)), _const((1, POOL_WIDTH)),
                  _const((CONV_WIDTH, CONV_WIDTH)), _const((D_MODEL, D_MODEL))],
        out_specs=row(D_MODEL), out_shape=jax.ShapeDtypeStruct((t, D_MODEL), F32),
        compiler_params=_params(("parallel",)))(
            h, omla, pooled, cact, lw['w_pool'], lw['pool_scale'], lw['w_conv_pw'], lw['w_out'])


def _mem_kv_kernel(mem_ref, g_ref, wk_ref, wv_ref, k_ref, v_ref):
    m = _rms(mem_ref[...], g_ref[...]).astype(wk_ref.dtype)
    k_ref[...] = _dot(m, wk_ref[...])
    v_ref[...] = _dot(m, wv_ref[...])


def _mem_kv(mem, lw):
    n = mem.shape[0]
    shp = jax.ShapeDtypeStruct((n, MEM_WIDTH), F32)
    return pl.pallas_call(_mem_kv_kernel, out_shape=[shp, shp],
                          compiler_params=_params(None))(mem, lw['g_mem_tok'], lw['w_mk'], lw['w_mv'])


def _cross_prompt_kernel(h_ref, g_ref, wq_ref, mk_ref, mv_ref, wo_ref, o_ref):
    cdt = wq_ref.dtype
    h = h_ref[...]
    q = _dot(_rms(h, g_ref[...]).astype(cdt), wq_ref[...]).astype(cdt)
    mk = mk_ref[...].astype(cdt)
    mv = mv_ref[...].astype(cdt)
    acc = h
    for hh in range(MEM_HEADS):
        c = slice(MEM_HEAD_DIM * hh, MEM_HEAD_DIM * (hh + 1))
        s = _dot_nt(q[:, c], mk[:, c])
        p = jnp.exp(s - jnp.max(s, axis=1, keepdims=True))
        p = p / jnp.sum(p, axis=1, keepdims=True)
        oh = _dot(p.astype(cdt), mv[:, c]).astype(cdt)
        acc = acc + _dot(oh, wo_ref[c, :])
    o_ref[...] = acc


def _cross_prompt(h, mk, mv, lw):
    t = h.shape[0]
    tm = min(t, 512)
    row = pl.BlockSpec((tm, D_MODEL), lambda i: (i, 0))
    return pl.pallas_call(
        _cross_prompt_kernel, grid=(t // tm,),
        in_specs=[row, _const((1, D_MODEL)), _const((D_MODEL, MEM_WIDTH)), _const((MEM_TOKENS, MEM_WIDTH)),
                  _const((MEM_TOKENS, MEM_WIDTH)), _const((MEM_WIDTH, D_MODEL))],
        out_specs=row, out_shape=jax.ShapeDtypeStruct((t, D_MODEL), F32),
        compiler_params=_params(("parallel",)))(h, lw['g_mem_x'], lw['w_mq'], mk, mv, lw['w_mo'])


_CROSS_SEQS = 8


def _cross_sample_kernel(h_ref, g_ref, wq_ref, k_ref, v_ref, wo_ref, o_ref):
    h = h_ref[...]
    q = _dot(_rms(h, g_ref[...]).astype(BF16), wq_ref[...])
    rows = 8
    lane_head = jnp.right_shift(lax.broadcasted_iota(jnp.int32, (rows, MEM_WIDTH), 1), MEM_HEAD_DIM.bit_length() - 1)
    row_id = lax.broadcasted_iota(jnp.int32, (rows, MEM_WIDTH), 0)
    head_mask = (lane_head == row_id).astype(F32)
    outs = []
    for b in range(_CROSS_SEQS):
        qb = (q[b:b + 1, :] * head_mask).astype(BF16)
        kb = k_ref[b].astype(BF16)
        vb = v_ref[b].astype(BF16)
        s = _dot_nt(qb, kb)
        p = jnp.exp(s - jnp.max(s, axis=1, keepdims=True))
        p = p / jnp.sum(p, axis=1, keepdims=True)
        ob = _dot(p.astype(BF16), vb) * head_mask
        outs.append(jnp.sum(ob, axis=0, keepdims=True))
    o = jnp.concatenate(outs, axis=0).astype(BF16)
    o_ref[...] = h + _dot(o, wo_ref[...])


def _cross_sample(layer, h, mem_k, mem_v, lw):
    s = h.shape[0]
    row = pl.BlockSpec((_CROSS_SEQS, D_MODEL), lambda i: (i, 0))
    kv = pl.BlockSpec((None, _CROSS_SEQS, MEM_TOKENS, MEM_WIDTH), lambda i: (layer, i, 0, 0))
    return pl.pallas_call(
        _cross_sample_kernel, grid=(s // _CROSS_SEQS,),
        in_specs=[row, _const((1, D_MODEL)), _const((D_MODEL, MEM_WIDTH)), kv, kv, _const((MEM_WIDTH, D_MODEL))],
        out_specs=row, out_shape=jax.ShapeDtypeStruct((s, D_MODEL), F32),
        compiler_params=_params(("parallel",)))(h, lw['g_mem_x'], lw['w_mq'], mem_k, mem_v, lw['w_mo'])


def _route_kernel(h_ref, g_ref, wr_ref, br_ref, xn_ref, rec_ref):
    xn = _rms(h_ref[...], g_ref[...])
    xn_ref[...] = xn.astype(BF16)
    logits = jnp.dot(xn, wr_ref[...], preferred_element_type=F32, precision=lax.Precision.HIGHEST) + br_ref[...]
    col = lax.broadcasted_iota(jnp.int32, logits.shape, 1)
    big = jnp.int32(ROUTE_LANES)
    is_group = col < N_GROUPS
    lg = jnp.where(is_group, logits, NEG_INF)
    g_max = jnp.max(lg, axis=1, keepdims=True)
    g_sel = jnp.min(jnp.where(lg == g_max, col, big), axis=1, keepdims=True)
    p_sel = 1.0 / jnp.sum(jnp.where(is_group, jnp.exp(lg - g_max), 0.0), axis=1, keepdims=True)
    e_col = col - N_GROUPS
    e_group = jnp.right_shift(e_col, EXPERTS_PER_GROUP.bit_length() - 1)
    in_group = (e_col >= 0) & (e_col < N_EXPERTS) & (e_group == g_sel)
    le = jnp.where(in_group, logits, NEG_INF)
    v1 = jnp.max(le, axis=1, keepdims=True)
    i1 = jnp.min(jnp.where(le == v1, col, big), axis=1, keepdims=True)
    le2 = jnp.where(col == i1, NEG_INF, le)
    v2 = jnp.max(le2, axis=1, keepdims=True)
    i2 = jnp.min(jnp.where(le2 == v2, col, big), axis=1, keepdims=True)
    e2 = jnp.exp(v2 - v1)
    w1 = p_sel / (1.0 + e2)
    w2 = p_sel * e2 / (1.0 + e2)
    rec = jnp.where(col == 0, (i1 - N_GROUPS).astype(F32),
                    jnp.where(col == 1, (i2 - N_GROUPS).astype(F32),
                              jnp.where(col == 2, w1, jnp.where(col == 3, w2, 0.0))))
    rec_ref[...] = rec


def _route(h, lw):
    t = h.shape[0]
    tm = min(t, 256)
    row = lambda w: pl.BlockSpec((tm, w), lambda i: (i, 0))
    return pl.pallas_call(
        _route_kernel, grid=(t // tm,),
        in_specs=[row(D_MODEL), _const((1, D_MODEL)), _const((D_MODEL, ROUTE_LANES)), _const((1, ROUTE_LANES))],
        out_specs=[row(D_MODEL), row(ROUTE_LANES)],
        out_shape=[jax.ShapeDtypeStruct((t, D_MODEL), BF16), jax.ShapeDtypeStruct((t, ROUTE_LANES), F32)],
        compiler_params=_params(("parallel",)))(h, lw['g_ffn'], lw['w_route'], lw['b_route'])


def _experts_kernel(te_ref, nt_ref, x_ref, wg_ref, wu_ref, wd_ref, o_ref):
    del te_ref

    @pl.when(pl.program_id(0) < nt_ref[0])
    def _():
        x = x_ref[...]
        hg = _dot(x, wg_ref[0])
        hu = _dot(x, wu_ref[0])
        act = (hg * jax.nn.sigmoid(hg) * hu).astype(BF16)
        o_ref[...] = _dot(act, wd_ref[0])


def _experts(tile_expert, n_tiles_used, x_sorted, lw):
    r = x_sorted.shape[0]
    n_tiles = r // MOE_TILE
    row = pl.BlockSpec((MOE_TILE, D_MODEL), lambda i, te, nt: (i, 0))
    grid_spec = pltpu.PrefetchScalarGridSpec(
        num_scalar_prefetch=2, grid=(n_tiles,),
        in_specs=[row,
                  pl.BlockSpec((1, D_MODEL, D_EXPERT), lambda i, te, nt: (te[i], 0, 0)),
                  pl.BlockSpec((1, D_MODEL, D_EXPERT), lambda i, te, nt: (te[i], 0, 0)),
                  pl.BlockSpec((1, D_EXPERT, D_MODEL), lambda i, te, nt: (te[i], 0, 0))],
        out_specs=row)
    return pl.pallas_call(
        _experts_kernel, grid_spec=grid_spec, out_shape=jax.ShapeDtypeStruct((r, D_MODEL), F32),
        compiler_params=_params(("arbitrary",)))(
            tile_expert, n_tiles_used, x_sorted, lw['w_gate'], lw['w_up'], lw['w_down'])


def _combine_kernel(h_ref, a_ref, b_ref, rec_ref, gf_ref, o_ref, *, final_norm):
    rec = rec_ref[...]
    y = h_ref[...] + rec[:, 2:3] * a_ref[...] + rec[:, 3:4] * b_ref[...]
    o_ref[...] = _rms(y, gf_ref[...]) if final_norm else y


def _combine(h, ya, yb, rec, g_final, *, final_norm):
    t = h.shape[0]
    tm = min(t, 512)
    row = lambda w: pl.BlockSpec((tm, w), lambda i: (i, 0))
    return pl.pallas_call(
        functools.partial(_combine_kernel, final_norm=final_norm), grid=(t // tm,),
        in_specs=[row(D_MODEL), row(D_MODEL), row(D_MODEL), row(ROUTE_LANES), _const((1, D_MODEL))],
        out_specs=row(D_MODEL), out_shape=jax.ShapeDtypeStruct((t, D_MODEL), F32),
        compiler_params=_params(("parallel",)))(h, ya, yb, rec, g_final)


def _dispatch_plan(expert_ids):
    n_pairs = expert_ids.shape[0]
    n_rows = -(-(n_pairs + N_EXPERTS * (MOE_TILE - 1)) // MOE_TILE) * MOE_TILE
    counts = jnp.zeros((N_EXPERTS,), jnp.int32).at[expert_ids].add(1)
    padded = -(-counts // MOE_TILE) * MOE_TILE
    ends = jnp.cumsum(padded)
    starts = ends - padded
    raw_starts = jnp.cumsum(counts) - counts
    order = jnp.argsort(expert_ids, stable=True).astype(jnp.int32)
    sorted_e = expert_ids[order]
    slot_sorted = starts[sorted_e] + jnp.arange(n_pairs, dtype=jnp.int32) - raw_starts[sorted_e]
    pair_slot = jnp.zeros((n_pairs,), jnp.int32).at[order].set(slot_sorted)
    slot_pair = jnp.zeros((n_rows,), jnp.int32).at[slot_sorted].set(order)
    tile_start = jnp.arange(n_rows // MOE_TILE, dtype=jnp.int32) * MOE_TILE
    tile_expert = jnp.minimum(jnp.searchsorted(ends, tile_start, side='right'), N_EXPERTS - 1).astype(jnp.int32)
    n_tiles_used = (ends[-1] // MOE_TILE).astype(jnp.int32).reshape(1)
    return pair_slot, slot_pair, tile_expert, n_tiles_used


def _moe(hp, hs, lw, g_final, *, final_norm, rec_tail=None):
    xn_p, rec_p = _route(hp, lw)
    xn_s, rec_s = _route(hs, lw)
    tp = hp.shape[0]
    if rec_tail is not None:
        rec_p = jnp.concatenate([rec_p[:tp - rec_tail.shape[0]], rec_tail], axis=0)
    rec = jnp.concatenate([rec_p, rec_s], axis=0)
    expert_ids = rec[:, :2].astype(jnp.int32).reshape(-1)
    pair_slot, slot_pair, tile_expert, n_tiles_used = _dispatch_plan(expert_ids)
    xn = jnp.concatenate([xn_p, xn_s], axis=0)
    x_sorted = jnp.take(xn, slot_pair // 2, axis=0)
    y_sorted = _experts(tile_expert, n_tiles_used, x_sorted, lw)
    slots = pair_slot.reshape(-1, 2)
    ya = jnp.take(y_sorted, slots[:, 0], axis=0)
    yb = jnp.take(y_sorted, slots[:, 1], axis=0)
    out_p = _combine(hp, ya[:tp], yb[:tp], rec_p, g_final, final_norm=final_norm)
    out_s = _combine(hs, ya[tp:], yb[tp:], rec_s, g_final, final_norm=final_norm)
    return out_p, out_s


_TAIL_ROWS = 32
_TAIL_CTX = 64


def _tail_in_kernel(h_ref, gmix_ref, w_ref, xpool_ref, xconv_ref):
    u = _dot(_rms(h_ref[...], gmix_ref[...]), w_ref[...])
    xpool_ref[...] = u[:, :POOL_WIDTH]
    xconv_ref[...] = u[:, POOL_WIDTH:POOL_WIDTH + CONV_WIDTH] * jax.nn.sigmoid(u[:, POOL_WIDTH + CONV_WIDTH:])


def _tail_route(l, w, lw, h_in, o_mla, mem):
    t = h_in.shape[0]
    h_ctx = h_in[t - _TAIL_CTX:]
    w_pool_glu = w['w_in'][l][:, Q_LORA + KV_LORA + ROPE_DIM:]
    shp = jax.ShapeDtypeStruct((_TAIL_CTX, POOL_WIDTH), F32)
    xpool, xconv = pl.pallas_call(_tail_in_kernel, out_shape=[shp, shp], compiler_params=_params(None))(
        h_ctx, lw['g_mix'], w_pool_glu)
    pooled, cact = _poolconv_prompt(xpool, xconv, lw, pos0=t - _TAIL_CTX, out_dtype=F32)
    lw32 = dict(lw, w_pool=w['w_pool'][l], w_conv_pw=w['w_conv_pw'][l], w_out=w['w_out'][l],
                w_mq=w['w_mq'][l] * MEM_SCALE, w_mk=w['w_mk'][l], w_mv=w['w_mv'][l], w_mo=w['w_mo'][l])
    keep = _TAIL_CTX - _TAIL_ROWS
    h1 = _mix_out(h_ctx[keep:], o_mla[t - _TAIL_ROWS:], pooled[keep:], cact[keep:], lw32)
    mk, mv = _mem_kv(mem, lw32)
    h2 = _cross_prompt(h1, mk, mv, lw32)
    return _route(h2, lw)[1]


def _rot_cols(w):
    half = ROPE_DIM // 2
    return jnp.concatenate([-w[..., half:], w[..., :half]], axis=-1)


def _rope_table(pos):
    half = ROPE_DIM // 2
    inv = ROPE_THETA ** (-jnp.arange(half, dtype=F32) / half)
    ang = pos.astype(F32)[:, None] * inv[None, :]
    cos, sin = jnp.cos(ang), jnp.sin(ang)
    return jnp.concatenate([cos, cos, sin, sin], axis=1)


def _layer_weights(l, w):
    o1, o2, o3, o4 = Q_LORA, Q_LORA + KV_LORA, Q_LORA + KV_LORA + ROPE_DIM, Q_LORA + KV_LORA + ROPE_DIM + POOL_WIDTH
    w_in = w['w_in'][l]
    w_kpe = w_in[:, o2:o3]
    w_in_r = jnp.concatenate([w_in[:, :o2], w_in[:, o3:], w_kpe, _rot_cols(w_kpe)], axis=1).astype(BF16)
    del o1, o4
    w_uq = w['w_uq'][l] * MLA_SCALE
    w_pe = w_uq[..., NOPE_DIM:]
    w_q = jnp.concatenate([w_uq[..., :NOPE_DIM], w_pe, _rot_cols(w_pe)], axis=-1)
    w_q = w_q.reshape(Q_LORA, MLA_HEADS * Q_CHUNK).astype(BF16)
    w_uk = w['w_uk'][l]
    w_uv = w['w_uv'][l]
    w_route = jnp.zeros((D_MODEL, ROUTE_LANES), F32)
    w_route = w_route.at[:, :N_GROUPS].set(w['w_route_group'][l]).at[:, N_GROUPS:N_GROUPS + N_EXPERTS].set(w['w_route_expert'][l])
    b_route = jnp.zeros((1, ROUTE_LANES), F32)
    b_route = b_route.at[0, :N_GROUPS].set(w['b_route_group'][l]).at[0, N_GROUPS:N_GROUPS + N_EXPERTS].set(w['b_route_expert'][l])
    vec = lambda name: w[name][l].reshape(1, -1)
    return {
        'g_mix': vec('g_mix'), 'w_in': w_in_r, 'g_q_lat': vec('g_q_lat'), 'w_q': w_q, 'g_kv_lat': vec('g_kv_lat'),
        'w_uk': w_uk.reshape(KV_LORA, MLA_HEADS * NOPE_DIM).astype(BF16),
        'w_uv': w_uv.reshape(KV_LORA, MLA_HEADS * V_DIM).astype(BF16),
        'w_uk_t': jnp.transpose(w_uk, (1, 2, 0)).astype(BF16),
        'w_uv_h': jnp.transpose(w_uv, (1, 0, 2)).astype(BF16),
        'w_pool': w['w_pool'][l].astype(BF16), 'pool_scale': vec('pool_scale'),
        'w_conv_dw': w['w_conv_dw'][l], 'b_conv_dw': vec('b_conv_dw'),
        'g_conv_ln': vec('g_conv_ln'), 'b_conv_ln': vec('b_conv_ln'),
        'w_conv_pw': w['w_conv_pw'][l].astype(BF16), 'w_out': w['w_out'][l].astype(BF16),
        'g_mem_x': vec('g_mem_x'), 'g_mem_tok': vec('g_mem_tok'),
        'w_mq': (w['w_mq'][l] * MEM_SCALE).astype(BF16), 'w_mk': w['w_mk'][l].astype(BF16),
        'w_mv': w['w_mv'][l].astype(BF16), 'w_mo': w['w_mo'][l].astype(BF16),
        'g_ffn': vec('g_ffn'), 'w_route': w_route, 'b_route': b_route,
        'w_gate': w['w_gate'][l].astype(BF16), 'w_up': w['w_up'][l].astype(BF16),
        'w_down': w['w_down'][l].astype(BF16),
    }


def kernel(x_prompt, x_sample, mem_prompt, cache_ckv, cache_kpe, page_table, state_pool, state_conv, cache_mem_k, cache_mem_v, g_mix, w_in, g_q_lat, w_uq, g_kv_lat, w_uk, w_uv, w_pool, pool_scale, w_conv_dw, b_conv_dw, g_conv_ln, b_conv_ln, w_conv_pw, w_out, g_mem_x, g_mem_tok, w_mq, w_mk, w_mv, w_mo, g_ffn, w_route_group, b_route_group, w_route_expert, b_route_expert, w_gate, w_up, w_down, g_final):
    weights = dict(g_mix=g_mix, w_in=w_in, g_q_lat=g_q_lat, w_uq=w_uq, g_kv_lat=g_kv_lat, w_uk=w_uk, w_uv=w_uv,
                   w_pool=w_pool, pool_scale=pool_scale, w_conv_dw=w_conv_dw, b_conv_dw=b_conv_dw,
                   g_conv_ln=g_conv_ln, b_conv_ln=b_conv_ln, w_conv_pw=w_conv_pw, w_out=w_out, g_mem_x=g_mem_x,
                   g_mem_tok=g_mem_tok, w_mq=w_mq, w_mk=w_mk, w_mv=w_mv, w_mo=w_mo, g_ffn=g_ffn,
                   w_route_group=w_route_group, b_route_group=b_route_group, w_route_expert=w_route_expert,
                   b_route_expert=b_route_expert, w_gate=w_gate, w_up=w_up, w_down=w_down)
    n_prompt, t_prompt = x_prompt.shape[:2]
    n_dec, s_dec = x_sample.shape[:2]
    assert n_prompt == 1 and s_dec == 1
    tab_p = _rope_table(jnp.arange(t_prompt, dtype=jnp.int32))
    tab_s = _rope_table(jnp.full((n_dec,), PAST_LEN, jnp.int32))
    hp = x_prompt.reshape(t_prompt, D_MODEL)
    hs = x_sample.reshape(n_dec, D_MODEL)
    mem = mem_prompt.reshape(MEM_TOKENS, D_MODEL)
    g_fin = g_final.reshape(1, D_MODEL)
    outs = {k: [] for k in ('ckv_p', 'kpe_p', 'pool_p', 'conv_p', 'mk_p', 'mv_p', 'ckv_s', 'kpe_s', 'pool_s', 'conv_s')}
    for l in range(DEPTH):
        lw = _layer_weights(l, weights)
        last = l == DEPTH - 1

        q, k, v, ckv, kpe, xpool, xconv = _mixer_in(hp, tab_p, lw, prompt=True)
        o_mla = _flash_attention(q, k, v)
        pooled, cact = _poolconv_prompt(xpool, xconv, lw)
        rec_tail = None if last else _tail_route(l, weights, lw, hp, o_mla, mem)
        hp = _mix_out(hp, o_mla, pooled, cact, lw)
        outs['ckv_p'].append(ckv.reshape(1, t_prompt, KV_LORA))
        outs['kpe_p'].append(kpe.reshape(1, t_prompt, ROPE_DIM))
        outs['pool_p'].append(xpool[-POOL_STATE:].reshape(1, POOL_STATE, POOL_WIDTH))
        outs['conv_p'].append(xconv[-CONV_STATE:].reshape(1, CONV_STATE, CONV_WIDTH))
        mk, mv = _mem_kv(mem, lw)
        outs['mk_p'].append(mk.reshape(1, MEM_TOKENS, MEM_HEADS, MEM_HEAD_DIM))
        outs['mv_p'].append(mv.reshape(1, MEM_TOKENS, MEM_HEADS, MEM_HEAD_DIM))
        hp = _cross_prompt(hp, mk, mv, lw)

        qlat, qpe, ckv_s, kpe_s, xpool_s, xconv_s = _mixer_in(hs, tab_s, lw, prompt=False)
        o_lat = _paged_attention(l, page_table, jnp.transpose(qlat, (1, 0, 2)), jnp.transpose(qpe, (1, 0, 2)),
                                 ckv_s.reshape(n_dec, 1, KV_LORA), kpe_s.reshape(n_dec, 1, ROPE_DIM),
                                 cache_ckv, cache_kpe)
        o_mla_s = _value_up(jnp.transpose(o_lat, (1, 0, 2)), lw['w_uv_h'])
        pooled_s, cact_s = _poolconv_sample(jnp.transpose(state_pool[l], (1, 0, 2)),
                                            jnp.transpose(state_conv[l], (1, 0, 2)), xpool_s, xconv_s, lw)
        hs = _mix_out(hs, o_mla_s, pooled_s, cact_s, lw)
        outs['ckv_s'].append(ckv_s.reshape(n_dec, 1, KV_LORA))
        outs['kpe_s'].append(kpe_s.reshape(n_dec, 1, ROPE_DIM))
        outs['pool_s'].append(jnp.concatenate([state_pool[l][:, 1:], xpool_s[:, None, :]], axis=1))
        outs['conv_s'].append(jnp.concatenate([state_conv[l][:, 1:], xconv_s[:, None, :]], axis=1))
        hs = _cross_sample(l, hs, cache_mem_k.reshape(DEPTH, n_dec, MEM_TOKENS, MEM_WIDTH),
                           cache_mem_v.reshape(DEPTH, n_dec, MEM_TOKENS, MEM_WIDTH), lw)

        hp, hs = _moe(hp, hs, lw, g_fin, final_norm=last, rec_tail=rec_tail)

    stack = lambda name: jnp.stack(outs[name])
    return (hp.reshape(1, t_prompt, D_MODEL), hs.reshape(n_dec, 1, D_MODEL),
            stack('ckv_p'), stack('kpe_p'), stack('pool_p'), stack('conv_p'), stack('mk_p'), stack('mv_p'),
            stack('ckv_s'), stack('kpe_s'), stack('pool_s'), stack('conv_s'))
```

```python
import functools

import jax
import jax.numpy as jnp
from jax import lax
from jax.experimental import pallas as pl
from jax.experimental.pallas import tpu as pltpu

F32 = jnp.float32
BF16 = jnp.bfloat16

D_MODEL = 2048
DEPTH = 2
PAST_LEN = 16384
PAGE_SIZE = 128
NOPE_DIM = 128
ROPE_DIM = 64
V_DIM = 128
MLA_HEADS = 8
Q_LORA = 768
KV_LORA = 256
ROPE_THETA = 10000.0
MLA_SCALE = (NOPE_DIM + ROPE_DIM) ** -0.5
LOG2_E = 1.4426950408889634
POOL_WIDTH = 512
CONV_WIDTH = 512
POOL_WINDOWS = (2, 4, 8, 16)
POOL_GROUP_WIDTH = 128
POOL_STATE = 15
CONV_K = 31
CONV_STATE = 30
MEM_TOKENS = 256
MEM_HEADS = 4
MEM_HEAD_DIM = 128
MEM_WIDTH = 512
MEM_SCALE = MEM_HEAD_DIM ** -0.5
N_GROUPS = 4
EXPERTS_PER_GROUP = 8
N_EXPERTS = 32
D_EXPERT = 512
EPS = 1e-6
NEG_INF = -1e30

_O_CQ = Q_LORA
_O_CKV = _O_CQ + KV_LORA
_O_POOL = _O_CKV + POOL_WIDTH
_O_GLU = _O_POOL + 2 * CONV_WIDTH
IN_COLS_R = _O_GLU + 2 * ROPE_DIM
Q_CHUNK = NOPE_DIM + 2 * ROPE_DIM

VMEM_LIMIT_V7X = 56 * 1024 * 1024
ROUTE_LANES = 128
PAGES_PER_STEP = 32
MIXER_ROWS = 256
FLASH_BLOCK = 512
FLASH_HEADS = 4
MOE_TILE = 256


def _params(sem):
    return pltpu.CompilerParams(dimension_semantics=sem, vmem_limit_bytes=VMEM_LIMIT_V7X)


def _const(shape):
    return pl.BlockSpec(shape, lambda *_: (0,) * len(shape), pipeline_mode=pl.Buffered(1))


def _rms(x, g):
    ms = jnp.mean(x * x, axis=-1, keepdims=True)
    return x * lax.rsqrt(ms + EPS) * g


def _precision(a):
    return lax.Precision.HIGHEST if a.dtype == F32 else None


def _dot(a, b):
    return jnp.dot(a, b, preferred_element_type=F32, precision=_precision(a))


def _dot_nt(a, b):
    return lax.dot_general(a, b, (((1,), (1,)), ((), ())), preferred_element_type=F32, precision=_precision(a))


def _mixer_common(h_ref, tab_ref, gmix_ref, win_ref, gq_ref, wq_ref, gkv_ref,
                  ckv_ref, kpe_ref, xpool_ref, xconv_ref):
    xn = _rms(h_ref[...], gmix_ref[...]).astype(BF16)
    u = _dot(xn, win_ref[...])
    cq = _rms(u[:, :_O_CQ], gq_ref[...]).astype(BF16)
    ckv = _rms(u[:, _O_CQ:_O_CKV], gkv_ref[...])
    ckv_ref[...] = ckv
    xpool_ref[...] = u[:, _O_CKV:_O_POOL]
    xconv_ref[...] = u[:, _O_POOL:_O_POOL + CONV_WIDTH] * jax.nn.sigmoid(u[:, _O_POOL + CONV_WIDTH:_O_GLU])
    tab = tab_ref[...]
    kp = u[:, _O_GLU:] * tab
    kp2 = kp + pltpu.roll(kp, ROPE_DIM, 1)
    kpe_ref[...] = kp2[:, :ROPE_DIM]
    qraw = _dot(cq, wq_ref[...])
    return ckv, kp2, qraw, tab


def _mixer_prompt_kernel(h_ref, tab_ref, gmix_ref, win_ref, gq_ref, wq_ref, gkv_ref, wuk_ref, wuv_ref,
                         q_ref, k_ref, v_ref, ckv_ref, kpe_ref, xpool_ref, xconv_ref):
    ckv, kp2, qraw, tab = _mixer_common(h_ref, tab_ref, gmix_ref, win_ref, gq_ref, wq_ref, gkv_ref,
                                        ckv_ref, kpe_ref, xpool_ref, xconv_ref)
    ckv_b = ckv.astype(BF16)
    kn = _dot(ckv_b, wuk_ref[...])
    vv = _dot(ckv_b, wuv_ref[...])
    kp2b = kp2.astype(BF16)
    for hh in range(MLA_HEADS):
        c0 = Q_CHUNK * hh
        q_ref[hh, :, :NOPE_DIM] = qraw[:, c0:c0 + NOPE_DIM].astype(BF16)
        q_ref[hh, :, NOPE_DIM:] = (qraw[:, c0 + NOPE_DIM:c0 + Q_CHUNK] * tab).astype(BF16)
        k_ref[hh, :, :NOPE_DIM] = kn[:, NOPE_DIM * hh:NOPE_DIM * (hh + 1)].astype(BF16)
        k_ref[hh, :, NOPE_DIM:] = kp2b
        v_ref[hh, 0] = vv[:, V_DIM * hh:V_DIM * (hh + 1)].T.astype(BF16)


def _mixer_sample_kernel(h_ref, tab_ref, gmix_ref, win_ref, gq_ref, wq_ref, gkv_ref, wukt_ref,
                         qlat_ref, qpe_ref, ckv_ref, kpe_ref, xpool_ref, xconv_ref):
    _, _, qraw, tab = _mixer_common(h_ref, tab_ref, gmix_ref, win_ref, gq_ref, wq_ref, gkv_ref,
                                    ckv_ref, kpe_ref, xpool_ref, xconv_ref)
    for hh in range(MLA_HEADS):
        c0 = Q_CHUNK * hh
        qn = qraw[:, c0:c0 + NOPE_DIM].astype(BF16)
        qlat_ref[hh] = _dot(qn, wukt_ref[hh]).astype(BF16)
        qp = qraw[:, c0 + NOPE_DIM:c0 + Q_CHUNK] * tab
        qp2 = qp + pltpu.roll(qp, ROPE_DIM, 1)
        qpe_ref[hh] = qp2[:, :ROPE_DIM].astype(BF16)


def _mixer_in(h, tab, lw, *, prompt):
    t = h.shape[0]
    tm = MIXER_ROWS if prompt else t
    row = lambda w: pl.BlockSpec((tm, w), lambda i: (i, 0))
    head = lambda w: pl.BlockSpec((MLA_HEADS, tm, w), lambda i: (0, i, 0))
    in_specs = [row(D_MODEL), row(2 * ROPE_DIM), _const((1, D_MODEL)), _const((D_MODEL, IN_COLS_R)),
                _const((1, Q_LORA)), _const((Q_LORA, MLA_HEADS * Q_CHUNK)), _const((1, KV_LORA))]
    args = [h, tab, lw['g_mix'], lw['w_in'], lw['g_q_lat'], lw['w_q'], lw['g_kv_lat']]
    tail_specs = [row(KV_LORA), row(ROPE_DIM), row(POOL_WIDTH), row(CONV_WIDTH)]
    tail_shapes = [jax.ShapeDtypeStruct((t, KV_LORA), F32), jax.ShapeDtypeStruct((t, ROPE_DIM), F32),
                   jax.ShapeDtypeStruct((t, POOL_WIDTH), F32), jax.ShapeDtypeStruct((t, CONV_WIDTH), F32)]
    if prompt:
        kern = _mixer_prompt_kernel
        in_specs += [_const((KV_LORA, MLA_HEADS * NOPE_DIM)), _const((KV_LORA, MLA_HEADS * V_DIM))]
        args += [lw['w_uk'], lw['w_uv']]
        vt_spec = pl.BlockSpec((MLA_HEADS, 1, V_DIM, tm), lambda i: (0, i, 0, 0))
        out_specs = [head(2 * NOPE_DIM), head(2 * NOPE_DIM), vt_spec] + tail_specs
        out_shape = [jax.ShapeDtypeStruct((MLA_HEADS, t, 2 * NOPE_DIM), BF16),
                     jax.ShapeDtypeStruct((MLA_HEADS, t, 2 * NOPE_DIM), BF16),
                     jax.ShapeDtypeStruct((MLA_HEADS, t // tm, V_DIM, tm), BF16)] + tail_shapes
    else:
        kern = _mixer_sample_kernel
        in_specs += [_const((MLA_HEADS, NOPE_DIM, KV_LORA))]
        args += [lw['w_uk_t']]
        out_specs = [head(KV_LORA), head(ROPE_DIM)] + tail_specs
        out_shape = [jax.ShapeDtypeStruct((MLA_HEADS, t, KV_LORA), BF16),
                     jax.ShapeDtypeStruct((MLA_HEADS, t, ROPE_DIM), BF16)] + tail_shapes
    return pl.pallas_call(kern, grid=(t // tm,), in_specs=in_specs, out_specs=out_specs,
                          out_shape=out_shape, compiler_params=_params(("parallel",)))(*args)


def _flash_kernel(q_ref, k_ref, vt_ref, o_ref, m_scr, l_scr, acc_scr):
    blk = FLASH_BLOCK
    sub = blk // MIXER_ROWS
    qi = pl.program_id(1)
    m_scr[...] = jnp.full_like(m_scr, -jnp.inf)
    l_scr[...] = jnp.zeros_like(l_scr)
    acc_scr[...] = jnp.zeros_like(acc_scr)

    def step(kv, diagonal):
        start = pl.multiple_of(kv * blk, blk)
        for hh in range(FLASH_HEADS):
            k = k_ref[hh, pl.ds(start, blk), :]
            st = _dot_nt(k, q_ref[hh])
            if diagonal:
                key = lax.broadcasted_iota(jnp.int32, (blk, blk), 0)
                qry = lax.broadcasted_iota(jnp.int32, (blk, blk), 1)
                st = jnp.where(key <= qry, st, NEG_INF)
            m_prev = m_scr[hh]
            m_new = jnp.maximum(m_prev, jnp.max(st, axis=0, keepdims=True))
            p = jnp.exp2(st - m_new)
            alpha = jnp.exp2(m_prev - m_new)
            l_scr[hh] = alpha * l_scr[hh] + jnp.sum(p, axis=0, keepdims=True)
            pb = p.astype(BF16)
            pv = _dot(vt_ref[hh, kv * sub], pb[:MIXER_ROWS])
            for c in range(1, sub):
                pv = pv + _dot(vt_ref[hh, kv * sub + c], pb[c * MIXER_ROWS:(c + 1) * MIXER_ROWS])
            acc_scr[hh] = alpha * acc_scr[hh] + pv
            m_scr[hh] = m_new

    def body(kv, carry):
        step(kv, False)
        return carry

    lax.fori_loop(0, qi, body, 0)
    step(qi, True)
    for hh in range(FLASH_HEADS):
        o_ref[:, V_DIM * hh:V_DIM * (hh + 1)] = (acc_scr[hh] / l_scr[hh]).T.astype(o_ref.dtype)


def _flash_attention(q, k, vt):
    nh, t, dk = q.shape
    blk, g = FLASH_BLOCK, FLASH_HEADS
    return pl.pallas_call(
        _flash_kernel,
        grid=(nh // g, t // blk),
        in_specs=[pl.BlockSpec((g, blk, dk), lambda h, i: (h, i, 0)),
                  pl.BlockSpec((g, t, dk), lambda h, i: (h, 0, 0), pipeline_mode=pl.Buffered(1)),
                  pl.BlockSpec((g, t // MIXER_ROWS, V_DIM, MIXER_ROWS), lambda h, i: (h, 0, 0, 0),
                               pipeline_mode=pl.Buffered(1))],
        out_specs=pl.BlockSpec((blk, g * V_DIM), lambda h, i: (i, h)),
        out_shape=jax.ShapeDtypeStruct((t, nh * V_DIM), F32),
        scratch_shapes=[pltpu.VMEM((g, 1, blk), F32), pltpu.VMEM((g, 1, blk), F32), pltpu.VMEM((g, V_DIM, blk), F32)],
        compiler_params=_params(("parallel", "parallel")))(q, k, vt)


def _paged_kernel(pt_ref, qlat_ref, qpe_ref, ckvn_ref, kpen_ref, *rest, n_steps):
    del pt_ref
    ckv_refs = rest[:PAGES_PER_STEP]
    kpe_refs = rest[PAGES_PER_STEP:2 * PAGES_PER_STEP]
    o_ref, m_scr, l_scr, acc_scr, ckv_buf, kpet_buf = rest[2 * PAGES_PER_STEP:]
    j = pl.program_id(1)
    ql = qlat_ref[0]
    qp = qpe_ref[0]

    @pl.when(j == 0)
    def _():
        cn = ckvn_ref[0]
        kn = kpen_ref[0]
        s_self = (jnp.sum(ql.astype(F32) * cn, axis=1, keepdims=True)
                  + jnp.sum(qp.astype(F32) * kn, axis=1, keepdims=True))
        m_scr[...] = s_self
        l_scr[...] = jnp.ones_like(l_scr)
        acc_scr[...] = jnp.broadcast_to(cn, acc_scr.shape)

    for i in range(PAGES_PER_STEP):
        ckv_buf[i * PAGE_SIZE:(i + 1) * PAGE_SIZE, :] = ckv_refs[i][...].astype(BF16)
        kpet_buf[:, i * PAGE_SIZE:(i + 1) * PAGE_SIZE] = kpe_refs[i][...].astype(BF16)
    kc = ckv_buf[...]
    s = _dot_nt(ql, kc) + _dot(qp, kpet_buf[...])
    m_prev = m_scr[...]
    m_new = jnp.maximum(m_prev, jnp.max(s, axis=1, keepdims=True))
    p = jnp.exp2(s - m_new)
    alpha = jnp.exp2(m_prev - m_new)
    l_scr[...] = alpha * l_scr[...] + jnp.sum(p, axis=1, keepdims=True)
    acc_scr[...] = alpha * acc_scr[...] + _dot(p.astype(BF16), kc)
    m_scr[...] = m_new

    @pl.when(j == n_steps - 1)
    def _():
        o_ref[0] = acc_scr[...] / l_scr[...]


def _paged_attention(layer, page_table, qlat, qpe, ckv_new, kpe_new, cache_ckv, cache_kpe_t):
    s, n_pages = page_table.shape
    n_steps = n_pages // PAGES_PER_STEP
    step_tokens = PAGES_PER_STEP * PAGE_SIZE

    def page_spec(rows, width, i):
        return pl.BlockSpec((None, None, rows, width),
                            lambda b, j, pt: (layer, pt[b, j * PAGES_PER_STEP + i], 0, 0))

    per_seq = lambda r, w: pl.BlockSpec((1, r, w), lambda b, j, pt: (b, 0, 0))
    in_specs = ([per_seq(MLA_HEADS, KV_LORA), per_seq(MLA_HEADS, ROPE_DIM), per_seq(1, KV_LORA), per_seq(1, ROPE_DIM)]
                + [page_spec(PAGE_SIZE, KV_LORA, i) for i in range(PAGES_PER_STEP)]
                + [page_spec(ROPE_DIM, PAGE_SIZE, i) for i in range(PAGES_PER_STEP)])
    grid_spec = pltpu.PrefetchScalarGridSpec(
        num_scalar_prefetch=1, grid=(s, n_steps), in_specs=in_specs,
        out_specs=per_seq(MLA_HEADS, KV_LORA),
        scratch_shapes=[pltpu.VMEM((MLA_HEADS, 1), F32), pltpu.VMEM((MLA_HEADS, 1), F32),
                        pltpu.VMEM((MLA_HEADS, KV_LORA), F32),
                        pltpu.VMEM((step_tokens, KV_LORA), BF16), pltpu.VMEM((ROPE_DIM, step_tokens), BF16)])
    return pl.pallas_call(
        functools.partial(_paged_kernel, n_steps=n_steps), grid_spec=grid_spec,
        out_shape=jax.ShapeDtypeStruct((s, MLA_HEADS, KV_LORA), F32),
        compiler_params=_params(("parallel", "arbitrary")))(
            page_table, qlat, qpe, ckv_new, kpe_new,
            *([cache_ckv] * PAGES_PER_STEP), *([cache_kpe_t] * PAGES_PER_STEP))


def _value_up_kernel(olat_ref, wuv_ref, o_ref):
    for hh in range(MLA_HEADS):
        o_ref[:, V_DIM * hh:V_DIM * (hh + 1)] = _dot(olat_ref[hh].astype(BF16), wuv_ref[hh]).astype(o_ref.dtype)


def _value_up(olat_h, w_uv_h):
    s = olat_h.shape[1]
    return pl.pallas_call(_value_up_kernel, out_shape=jax.ShapeDtypeStruct((s, MLA_HEADS * V_DIM), BF16))(olat_h, w_uv_h)


def _ln_silu(y, g, b):
    mu = jnp.mean(y, axis=-1, keepdims=True)
    d = y - mu
    var = jnp.mean(d * d, axis=-1, keepdims=True)
    z = d * lax.rsqrt(var + EPS) * g + b
    return z * jax.nn.sigmoid(z)


_POOL_PAD = 16
_CONV_PAD = 32


def _poolconv_prompt_kernel(xp_ref, xc_ref, wdw_ref, bdw_ref, gln_ref, bln_ref, pooled_ref, cact_ref,
                            extp, extc, *, tm, pos0):
    i = pl.program_id(0)

    @pl.when(i == 0)
    def _():
        extp[0:_POOL_PAD, :] = jnp.zeros((_POOL_PAD, POOL_WIDTH), F32)
        extc[0:_CONV_PAD, :] = jnp.zeros((_CONV_PAD, CONV_WIDTH), F32)

    extp[_POOL_PAD:, :] = xp_ref[...]
    extc[_CONV_PAD:, :] = xc_ref[...]
    pos = pos0 + i * tm + lax.broadcasted_iota(jnp.int32, (tm, 1), 0)
    for gi, w in enumerate(POOL_WINDOWS):
        c0 = gi * POOL_GROUP_WIDTH
        x_new = extp[_POOL_PAD:, c0:c0 + POOL_GROUP_WIDTH]
        acc = x_new
        for k in range(1, w):
            acc = acc + extp[_POOL_PAD - k:_POOL_PAD - k + tm, c0:c0 + POOL_GROUP_WIDTH]
        cnt = jnp.minimum(pos + 1, w).astype(F32)
        pooled_ref[:, c0:c0 + POOL_GROUP_WIDTH] = (acc / cnt - x_new).astype(pooled_ref.dtype)
    base = _CONV_PAD - CONV_STATE
    acc = extc[base:base + tm, :] * wdw_ref[0:1, :]
    for k in range(1, CONV_K):
        acc = acc + extc[base + k:base + k + tm, :] * wdw_ref[k:k + 1, :]
    cact_ref[...] = _ln_silu(acc + bdw_ref[...], gln_ref[...], bln_ref[...]).astype(cact_ref.dtype)
    extp[0:_POOL_PAD, :] = extp[tm:tm + _POOL_PAD, :]
    extc[0:_CONV_PAD, :] = extc[tm:tm + _CONV_PAD, :]


def _poolconv_prompt(xpool, xconv, lw, *, pos0=0, out_dtype=BF16):
    t = xpool.shape[0]
    tm = min(t, 512)
    row = pl.BlockSpec((tm, POOL_WIDTH), lambda i: (i, 0))
    vec = _const((1, CONV_WIDTH))
    return pl.pallas_call(
        functools.partial(_poolconv_prompt_kernel, tm=tm, pos0=pos0), grid=(t // tm,),
        in_specs=[row, row, _const((CONV_K, CONV_WIDTH)), vec, vec, vec],
        out_specs=[row, row],
        out_shape=[jax.ShapeDtypeStruct((t, POOL_WIDTH), out_dtype), jax.ShapeDtypeStruct((t, CONV_WIDTH), out_dtype)],
        scratch_shapes=[pltpu.VMEM((tm + _POOL_PAD, POOL_WIDTH), F32), pltpu.VMEM((tm + _CONV_PAD, CONV_WIDTH), F32)],
        compiler_params=_params(("arbitrary",)))(
            xpool, xconv, lw['w_conv_dw'], lw['b_conv_dw'], lw['g_conv_ln'], lw['b_conv_ln'])


def _poolconv_sample_kernel(sp_ref, sc_ref, xp_ref, xc_ref, wdw_ref, bdw_ref, gln_ref, bln_ref,
                            pooled_ref, cact_ref):
    for gi, w in enumerate(POOL_WINDOWS):
        c0 = gi * POOL_GROUP_WIDTH
        x_new = xp_ref[:, c0:c0 + POOL_GROUP_WIDTH]
        acc = x_new
        for k in range(1, w):
            acc = acc + sp_ref[POOL_STATE - k, :, c0:c0 + POOL_GROUP_WIDTH]
        cnt = float(min(PAST_LEN + 1, w))
        pooled_ref[:, c0:c0 + POOL_GROUP_WIDTH] = (acc / cnt - x_new).astype(pooled_ref.dtype)
    acc = xc_ref[...] * wdw_ref[CONV_K - 1:CONV_K, :]
    for k in range(CONV_STATE):
        acc = acc + sc_ref[k] * wdw_ref[k:k + 1, :]
    cact_ref[...] = _ln_silu(acc + bdw_ref[...], gln_ref[...], bln_ref[...]).astype(cact_ref.dtype)


def _poolconv_sample(state_pool_t, state_conv_t, xpool, xconv, lw):
    s = xpool.shape[0]
    return pl.pallas_call(
        _poolconv_sample_kernel,
        out_shape=[jax.ShapeDtypeStruct((s, POOL_WIDTH), BF16), jax.ShapeDtypeStruct((s, CONV_WIDTH), BF16)])(
            state_pool_t, state_conv_t, xpool, xconv,
            lw['w_conv_dw'], lw['b_conv_dw'], lw['g_conv_ln'], lw['b_conv_ln'])


def _mix_out_kernel(h_ref, omla_ref, pooled_ref, cact_ref, wpool_ref, pscale_ref, wpw_ref, wout_ref, o_ref):
    cdt = wout_ref.dtype
    acc = h_ref[...] + _dot(omla_ref[...].astype(cdt), wout_ref[0:1024, :])
    pooled = pooled_ref[...]
    for gi in range(len(POOL_WINDOWS)):
        c0 = gi * POOL_GROUP_WIDTH
        og = _dot(pooled[:, c0:c0 + POOL_GROUP_WIDTH], wpool_ref[gi]) * pscale_ref[:, c0:c0 + POOL_GROUP_WIDTH]
        acc = acc + _dot(og.astype(cdt), wout_ref[1024 + c0:1024 + c0 + POOL_GROUP_WIDTH, :])
    oc = _dot(cact_ref[...], wpw_ref[...]).astype(cdt)
    o_ref[...] = acc + _dot(oc, wout_ref[1536:2048, :])


def _mix_out(h, omla, pooled, cact, lw):
    t = h.shape[0]
    tm = min(t, 512)
    row = lambda w: pl.BlockSpec((tm, w), lambda i: (i, 0))
    return pl.pallas_call(
        _mix_out_kernel, grid=(t // tm,),
        in_specs=[row(D_MODEL), row(1024), row(POOL_WIDTH), row(CONV_WIDTH),
                  _const((len(POOL_WINDOWS), POOL_GROUP_WIDTH, POOL_GROUP_WIDTH)), _const((1, POOL_WIDTH)),
                  _const((CONV_WIDTH, CONV_WIDTH)), _const((D_MODEL, D_MODEL))],
        out_specs=row(D_MODEL), out_shape=jax.ShapeDtypeStruct((t, D_MODEL), F32),
        compiler_params=_params(("parallel",)))(
            h, omla, pooled, cact, lw['w_pool'], lw['pool_scale'], lw['w_conv_pw'], lw['w_out'])


def _mem_kv_kernel(mem_ref, g_ref, wk_ref, wv_ref, k_ref, v_ref):
    m = _rms(mem_ref[...], g_ref[...]).astype(wk_ref.dtype)
    k_ref[...] = _dot(m, wk_ref[...])
    v_ref[...] = _dot(m, wv_ref[...])


def _mem_kv(mem, lw):
    n = mem.shape[0]
    shp = jax.ShapeDtypeStruct((n, MEM_WIDTH), F32)
    return pl.pallas_call(_mem_kv_kernel, out_shape=[shp, shp],
                          compiler_params=_params(None))(mem, lw['g_mem_tok'], lw['w_mk'], lw['w_mv'])


def _cross_prompt_kernel(h_ref, g_ref, wq_ref, mk_ref, mv_ref, wo_ref, o_ref):
    cdt = wq_ref.dtype
    h = h_ref[...]
    q = _dot(_rms(h, g_ref[...]).astype(cdt), wq_ref[...]).astype(cdt)
    mk = mk_ref[...].astype(cdt)
    mv = mv_ref[...].astype(cdt)
    acc = h
    for hh in range(MEM_HEADS):
        c = slice(MEM_HEAD_DIM * hh, MEM_HEAD_DIM * (hh + 1))
        s = _dot_nt(q[:, c], mk[:, c])
        p = jnp.exp(s - jnp.max(s, axis=1, keepdims=True))
        p = p / jnp.sum(p, axis=1, keepdims=True)
        oh = _dot(p.astype(cdt), mv[:, c]).astype(cdt)
        acc = acc + _dot(oh, wo_ref[c, :])
    o_ref[...] = acc


def _cross_prompt(h, mk, mv, lw):
    t = h.shape[0]
    tm = min(t, 512)
    row = pl.BlockSpec((tm, D_MODEL), lambda i: (i, 0))
    return pl.pallas_call(
        _cross_prompt_kernel, grid=(t // tm,),
        in_specs=[row, _const((1, D_MODEL)), _const((D_MODEL, MEM_WIDTH)), _const((MEM_TOKENS, MEM_WIDTH)),
                  _const((MEM_TOKENS, MEM_WIDTH)), _const((MEM_WIDTH, D_MODEL))],
        out_specs=row, out_shape=jax.ShapeDtypeStruct((t, D_MODEL), F32),
        compiler_params=_params(("parallel",)))(h, lw['g_mem_x'], lw['w_mq'], mk, mv, lw['w_mo'])


_CROSS_SEQS = 8


def _cross_sample_kernel(h_ref, g_ref, wq_ref, k_ref, v_ref, wo_ref, o_ref):
    h = h_ref[...]
    q = _dot(_rms(h, g_ref[...]).astype(BF16), wq_ref[...])
    rows = 8
    n_rows = MEM_TOKENS * MEM_HEADS
    col_head = jnp.bitwise_and(lax.broadcasted_iota(jnp.int32, (rows, n_rows), 1), MEM_HEADS - 1)
    own = col_head == lax.broadcasted_iota(jnp.int32, (rows, n_rows), 0)
    pad = jnp.zeros((rows - MEM_HEADS, MEM_HEAD_DIM), F32)
    outs = []
    for b in range(_CROSS_SEQS):
        qb = jnp.concatenate([q[b:b + 1, MEM_HEAD_DIM * hh:MEM_HEAD_DIM * (hh + 1)] for hh in range(MEM_HEADS)]
                             + [pad], axis=0).astype(BF16)
        kb = k_ref[b].astype(BF16)
        vb = v_ref[b].astype(BF16)
        s = jnp.where(own, _dot_nt(qb, kb), NEG_INF)
        p = jnp.exp(s - jnp.max(s, axis=1, keepdims=True))
        p = p / jnp.sum(p, axis=1, keepdims=True)
        ob = _dot(p.astype(BF16), vb)
        outs.append(jnp.concatenate([ob[hh:hh + 1, :] for hh in range(MEM_HEADS)], axis=1))
    o = jnp.concatenate(outs, axis=0).astype(BF16)
    o_ref[...] = h + _dot(o, wo_ref[...])


def _cross_sample(layer, h, mem_k, mem_v, lw):
    s = h.shape[0]
    row = pl.BlockSpec((_CROSS_SEQS, D_MODEL), lambda i: (i, 0))
    kv = pl.BlockSpec((None, _CROSS_SEQS, MEM_TOKENS * MEM_HEADS, MEM_HEAD_DIM), lambda i: (layer, i, 0, 0))
    return pl.pallas_call(
        _cross_sample_kernel, grid=(s // _CROSS_SEQS,),
        in_specs=[row, _const((1, D_MODEL)), _const((D_MODEL, MEM_WIDTH)), kv, kv, _const((MEM_WIDTH, D_MODEL))],
        out_specs=row, out_shape=jax.ShapeDtypeStruct((s, D_MODEL), F32),
        compiler_params=_params(("parallel",)))(h, lw['g_mem_x'], lw['w_mq'], mem_k, mem_v, lw['w_mo'])


def _route_kernel(h_ref, g_ref, wr_ref, br_ref, xn_ref, rec_ref):
    xn = _rms(h_ref[...], g_ref[...])
    xn_ref[...] = xn.astype(BF16)
    logits = jnp.dot(xn, wr_ref[...], preferred_element_type=F32, precision=lax.Precision.HIGHEST) + br_ref[...]
    col = lax.broadcasted_iota(jnp.int32, logits.shape, 1)
    big = jnp.int32(ROUTE_LANES)
    is_group = col < N_GROUPS
    lg = jnp.where(is_group, logits, NEG_INF)
    g_max = jnp.max(lg, axis=1, keepdims=True)
    g_sel = jnp.min(jnp.where(lg == g_max, col, big), axis=1, keepdims=True)
    p_sel = 1.0 / jnp.sum(jnp.where(is_group, jnp.exp(lg - g_max), 0.0), axis=1, keepdims=True)
    e_col = col - N_GROUPS
    e_group = jnp.right_shift(e_col, EXPERTS_PER_GROUP.bit_length() - 1)
    in_group = (e_col >= 0) & (e_col < N_EXPERTS) & (e_group == g_sel)
    le = jnp.where(in_group, logits, NEG_INF)
    v1 = jnp.max(le, axis=1, keepdims=True)
    i1 = jnp.min(jnp.where(le == v1, col, big), axis=1, keepdims=True)
    le2 = jnp.where(col == i1, NEG_INF, le)
    v2 = jnp.max(le2, axis=1, keepdims=True)
    i2 = jnp.min(jnp.where(le2 == v2, col, big), axis=1, keepdims=True)
    e2 = jnp.exp(v2 - v1)
    w1 = p_sel / (1.0 + e2)
    w2 = p_sel * e2 / (1.0 + e2)
    rec = jnp.where(col == 0, (i1 - N_GROUPS).astype(F32),
                    jnp.where(col == 1, (i2 - N_GROUPS).astype(F32),
                              jnp.where(col == 2, w1, jnp.where(col == 3, w2, 0.0))))
    rec_ref[...] = rec


def _route(h, lw):
    t = h.shape[0]
    tm = min(t, 256)
    row = lambda w: pl.BlockSpec((tm, w), lambda i: (i, 0))
    return pl.pallas_call(
        _route_kernel, grid=(t // tm,),
        in_specs=[row(D_MODEL), _const((1, D_MODEL)), _const((D_MODEL, ROUTE_LANES)), _const((1, ROUTE_LANES))],
        out_specs=[row(D_MODEL), row(ROUTE_LANES)],
        out_shape=[jax.ShapeDtypeStruct((t, D_MODEL), BF16), jax.ShapeDtypeStruct((t, ROUTE_LANES), F32)],
        compiler_params=_params(("parallel",)))(h, lw['g_ffn'], lw['w_route'], lw['b_route'])


def _experts_kernel(te_ref, nt_ref, x_ref, wg_ref, wu_ref, wd_ref, o_ref, wg_b, wu_b, wd_b):
    i = pl.program_id(0)
    used = i < nt_ref[0]
    new_expert = (i == 0) | (te_ref[i] != te_ref[jnp.maximum(i - 1, 0)])

    @pl.when(used & new_expert)
    def _():
        wg_b[...] = wg_ref[...].astype(BF16)
        wu_b[...] = wu_ref[...].astype(BF16)
        wd_b[...] = wd_ref[...].astype(BF16)

    @pl.when(used)
    def _():
        x = x_ref[...]
        hg = _dot(x, wg_b[...])
        hu = _dot(x, wu_b[...])
        act = (hg * jax.nn.sigmoid(hg) * hu).astype(BF16)
        o_ref[...] = _dot(act, wd_b[...])


def _experts(layer, tile_expert, n_tiles_used, x_sorted, w_gate, w_up, w_down):
    r = x_sorted.shape[0]
    n_tiles = r // MOE_TILE
    row = pl.BlockSpec((MOE_TILE, D_MODEL), lambda i, te, nt: (i, 0))
    w_spec = lambda a, b: pl.BlockSpec((None, None, a, b), lambda i, te, nt: (layer, te[i], 0, 0))
    grid_spec = pltpu.PrefetchScalarGridSpec(
        num_scalar_prefetch=2, grid=(n_tiles,),
        in_specs=[row, w_spec(D_MODEL, D_EXPERT), w_spec(D_MODEL, D_EXPERT), w_spec(D_EXPERT, D_MODEL)],
        out_specs=row,
        scratch_shapes=[pltpu.VMEM((D_MODEL, D_EXPERT), BF16), pltpu.VMEM((D_MODEL, D_EXPERT), BF16),
                        pltpu.VMEM((D_EXPERT, D_MODEL), BF16)])
    return pl.pallas_call(
        _experts_kernel, grid_spec=grid_spec, out_shape=jax.ShapeDtypeStruct((r, D_MODEL), F32),
        compiler_params=_params(("arbitrary",)))(tile_expert, n_tiles_used, x_sorted, w_gate, w_up, w_down)


def _combine_kernel(h_ref, a_ref, b_ref, rec_ref, gf_ref, o_ref, *, final_norm):
    rec = rec_ref[...]
    y = h_ref[...] + rec[:, 2:3] * a_ref[...] + rec[:, 3:4] * b_ref[...]
    o_ref[...] = _rms(y, gf_ref[...]) if final_norm else y


def _combine(h, ya, yb, rec, g_final, *, final_norm):
    t = h.shape[0]
    tm = min(t, 512)
    row = lambda w: pl.BlockSpec((tm, w), lambda i: (i, 0))
    return pl.pallas_call(
        functools.partial(_combine_kernel, final_norm=final_norm), grid=(t // tm,),
        in_specs=[row(D_MODEL), row(D_MODEL), row(D_MODEL), row(ROUTE_LANES), _const((1, D_MODEL))],
        out_specs=row(D_MODEL), out_shape=jax.ShapeDtypeStruct((t, D_MODEL), F32),
        compiler_params=_params(("parallel",)))(h, ya, yb, rec, g_final)


def _dispatch_plan(expert_ids):
    n_pairs = expert_ids.shape[0]
    n_rows = -(-(n_pairs + N_EXPERTS * (MOE_TILE - 1)) // MOE_TILE) * MOE_TILE
    one_hot = (expert_ids[:, None] == jnp.arange(N_EXPERTS, dtype=jnp.int32)[None, :]).astype(jnp.int32)
    running = jnp.cumsum(one_hot, axis=0)
    rank = jnp.sum(running * one_hot, axis=1) - 1
    counts = running[-1]
    padded = -(-counts // MOE_TILE) * MOE_TILE
    ends = jnp.cumsum(padded)
    starts = ends - padded
    pair_slot = jnp.sum(one_hot * starts[None, :], axis=1) + rank
    slot_pair = jnp.zeros((n_rows,), jnp.int32).at[pair_slot].set(jnp.arange(n_pairs, dtype=jnp.int32),
                                                                   unique_indices=True)
    tile_start = jnp.arange(n_rows // MOE_TILE, dtype=jnp.int32) * MOE_TILE
    tile_expert = jnp.sum((tile_start[:, None] >= ends[None, :]).astype(jnp.int32), axis=1)
    tile_expert = jnp.minimum(tile_expert, N_EXPERTS - 1)
    n_tiles_used = (ends[-1] // MOE_TILE).astype(jnp.int32).reshape(1)
    return pair_slot, slot_pair, tile_expert, n_tiles_used


def _moe(layer, hp, hs, lw, w, g_final, *, final_norm, rec_tail=None):
    xn_p, rec_p = _route(hp, lw)
    xn_s, rec_s = _route(hs, lw)
    tp = hp.shape[0]
    if rec_tail is not None:
        rec_p = jnp.concatenate([rec_p[:tp - rec_tail.shape[0]], rec_tail], axis=0)
    rec = jnp.concatenate([rec_p, rec_s], axis=0)
    expert_ids = rec[:, :2].astype(jnp.int32).reshape(-1)
    pair_slot, slot_pair, tile_expert, n_tiles_used = _dispatch_plan(expert_ids)
    xn = jnp.concatenate([xn_p, xn_s], axis=0)
    x_sorted = jnp.take(xn, slot_pair // 2, axis=0, mode='clip')
    y_sorted = _experts(layer, tile_expert, n_tiles_used, x_sorted, w['w_gate'], w['w_up'], w['w_down'])
    slots = pair_slot.reshape(-1, 2)
    ya = jnp.take(y_sorted, slots[:, 0], axis=0, mode='clip')
    yb = jnp.take(y_sorted, slots[:, 1], axis=0, mode='clip')
    out_p = _combine(hp, ya[:tp], yb[:tp], rec_p, g_final, final_norm=final_norm)
    out_s = _combine(hs, ya[tp:], yb[tp:], rec_s, g_final, final_norm=final_norm)
    return out_p, out_s


_TAIL_ROWS = 32
_TAIL_CTX = 64


def _tail_in_kernel(h_ref, gmix_ref, w_ref, xpool_ref, xconv_ref):
    u = _dot(_rms(h_ref[...], gmix_ref[...]), w_ref[...])
    xpool_ref[...] = u[:, :POOL_WIDTH]
    xconv_ref[...] = u[:, POOL_WIDTH:POOL_WIDTH + CONV_WIDTH] * jax.nn.sigmoid(u[:, POOL_WIDTH + CONV_WIDTH:])


def _tail_route(l, w, lw, h_in, o_mla, mem):
    t = h_in.shape[0]
    h_ctx = h_in[t - _TAIL_CTX:]
    w_pool_glu = w['w_in'][l][:, Q_LORA + KV_LORA + ROPE_DIM:]
    shp = jax.ShapeDtypeStruct((_TAIL_CTX, POOL_WIDTH), F32)
    xpool, xconv = pl.pallas_call(_tail_in_kernel, out_shape=[shp, shp], compiler_params=_params(None))(
        h_ctx, lw['g_mix'], w_pool_glu)
    pooled, cact = _poolconv_prompt(xpool, xconv, lw, pos0=t - _TAIL_CTX, out_dtype=F32)
    lw32 = dict(lw, w_pool=w['w_pool'][l], w_conv_pw=w['w_conv_pw'][l], w_out=w['w_out'][l],
                w_mq=w['w_mq'][l] * MEM_SCALE, w_mk=w['w_mk'][l], w_mv=w['w_mv'][l], w_mo=w['w_mo'][l])
    keep = _TAIL_CTX - _TAIL_ROWS
    h1 = _mix_out(h_ctx[keep:], o_mla[t - _TAIL_ROWS:], pooled[keep:], cact[keep:], lw32)
    mk, mv = _mem_kv(mem, lw32)
    h2 = _cross_prompt(h1, mk, mv, lw32)
    return _route(h2, lw)[1]


def _rot_cols(w):
    half = ROPE_DIM // 2
    return jnp.concatenate([-w[..., half:], w[..., :half]], axis=-1)


def _rope_table(pos):
    half = ROPE_DIM // 2
    inv = ROPE_THETA ** (-jnp.arange(half, dtype=F32) / half)
    ang = pos.astype(F32)[:, None] * inv[None, :]
    cos, sin = jnp.cos(ang), jnp.sin(ang)
    return jnp.concatenate([cos, cos, sin, sin], axis=1)


def _layer_weights(l, w):
    o1, o2, o3, o4 = Q_LORA, Q_LORA + KV_LORA, Q_LORA + KV_LORA + ROPE_DIM, Q_LORA + KV_LORA + ROPE_DIM + POOL_WIDTH
    w_in = w['w_in'][l]
    w_kpe = w_in[:, o2:o3]
    w_in_r = jnp.concatenate([w_in[:, :o2], w_in[:, o3:], w_kpe, _rot_cols(w_kpe)], axis=1).astype(BF16)
    del o1, o4
    w_uq = w['w_uq'][l] * (MLA_SCALE * LOG2_E)
    w_pe = w_uq[..., NOPE_DIM:]
    w_q = jnp.concatenate([w_uq[..., :NOPE_DIM], w_pe, _rot_cols(w_pe)], axis=-1)
    w_q = w_q.reshape(Q_LORA, MLA_HEADS * Q_CHUNK).astype(BF16)
    w_uk = w['w_uk'][l]
    w_uv = w['w_uv'][l]
    w_route = jnp.zeros((D_MODEL, ROUTE_LANES), F32)
    w_route = w_route.at[:, :N_GROUPS].set(w['w_route_group'][l]).at[:, N_GROUPS:N_GROUPS + N_EXPERTS].set(w['w_route_expert'][l])
    b_route = jnp.zeros((1, ROUTE_LANES), F32)
    b_route = b_route.at[0, :N_GROUPS].set(w['b_route_group'][l]).at[0, N_GROUPS:N_GROUPS + N_EXPERTS].set(w['b_route_expert'][l])
    vec = lambda name: w[name][l].reshape(1, -1)
    return {
        'g_mix': vec('g_mix'), 'w_in': w_in_r, 'g_q_lat': vec('g_q_lat'), 'w_q': w_q, 'g_kv_lat': vec('g_kv_lat'),
        'w_uk': w_uk.reshape(KV_LORA, MLA_HEADS * NOPE_DIM).astype(BF16),
        'w_uv': w_uv.reshape(KV_LORA, MLA_HEADS * V_DIM).astype(BF16),
        'w_uk_t': jnp.transpose(w_uk, (1, 2, 0)).astype(BF16),
        'w_uv_h': jnp.transpose(w_uv, (1, 0, 2)).astype(BF16),
        'w_pool': w['w_pool'][l].astype(BF16), 'pool_scale': vec('pool_scale'),
        'w_conv_dw': w['w_conv_dw'][l], 'b_conv_dw': vec('b_conv_dw'),
        'g_conv_ln': vec('g_conv_ln'), 'b_conv_ln': vec('b_conv_ln'),
        'w_conv_pw': w['w_conv_pw'][l].astype(BF16), 'w_out': w['w_out'][l].astype(BF16),
        'g_mem_x': vec('g_mem_x'), 'g_mem_tok': vec('g_mem_tok'),
        'w_mq': (w['w_mq'][l] * MEM_SCALE).astype(BF16), 'w_mk': w['w_mk'][l].astype(BF16),
        'w_mv': w['w_mv'][l].astype(BF16), 'w_mo': w['w_mo'][l].astype(BF16),
        'g_ffn': vec('g_ffn'), 'w_route': w_route, 'b_route': b_route,
    }


def kernel(x_prompt, x_sample, mem_prompt, cache_ckv, cache_kpe, page_table, state_pool, state_conv, cache_mem_k, cache_mem_v, g_mix, w_in, g_q_lat, w_uq, g_kv_lat, w_uk, w_uv, w_pool, pool_scale, w_conv_dw, b_conv_dw, g_conv_ln, b_conv_ln, w_conv_pw, w_out, g_mem_x, g_mem_tok, w_mq, w_mk, w_mv, w_mo, g_ffn, w_route_group, b_route_group, w_route_expert, b_route_expert, w_gate, w_up, w_down, g_final):
    weights = dict(g_mix=g_mix, w_in=w_in, g_q_lat=g_q_lat, w_uq=w_uq, g_kv_lat=g_kv_lat, w_uk=w_uk, w_uv=w_uv,
                   w_pool=w_pool, pool_scale=pool_scale, w_conv_dw=w_conv_dw, b_conv_dw=b_conv_dw,
                   g_conv_ln=g_conv_ln, b_conv_ln=b_conv_ln, w_conv_pw=w_conv_pw, w_out=w_out, g_mem_x=g_mem_x,
                   g_mem_tok=g_mem_tok, w_mq=w_mq, w_mk=w_mk, w_mv=w_mv, w_mo=w_mo, g_ffn=g_ffn,
                   w_route_group=w_route_group, b_route_group=b_route_group, w_route_expert=w_route_expert,
                   b_route_expert=b_route_expert, w_gate=w_gate, w_up=w_up, w_down=w_down)
    n_prompt, t_prompt = x_prompt.shape[:2]
    n_dec, s_dec = x_sample.shape[:2]
    assert n_prompt == 1 and s_dec == 1
    tab_p = _rope_table(jnp.arange(t_prompt, dtype=jnp.int32))
    tab_s = _rope_table(jnp.full((n_dec,), PAST_LEN, jnp.int32))
    hp = x_prompt.reshape(t_prompt, D_MODEL)
    hs = x_sample.reshape(n_dec, D_MODEL)
    mem = mem_prompt.reshape(MEM_TOKENS, D_MODEL)
    g_fin = g_final.reshape(1, D_MODEL)
    cache_kpe_t = jnp.swapaxes(cache_kpe, 2, 3)
    mem_k_rows = cache_mem_k.reshape(DEPTH, n_dec, MEM_TOKENS * MEM_HEADS, MEM_HEAD_DIM)
    mem_v_rows = cache_mem_v.reshape(DEPTH, n_dec, MEM_TOKENS * MEM_HEADS, MEM_HEAD_DIM)
    outs = {k: [] for k in ('ckv_p', 'kpe_p', 'pool_p', 'conv_p', 'mk_p', 'mv_p', 'ckv_s', 'kpe_s', 'pool_s', 'conv_s')}
    for l in range(DEPTH):
        lw = _layer_weights(l, weights)
        last = l == DEPTH - 1

        q, k, v, ckv, kpe, xpool, xconv = _mixer_in(hp, tab_p, lw, prompt=True)
        o_mla = _flash_attention(q, k, v)
        pooled, cact = _poolconv_prompt(xpool, xconv, lw)
        rec_tail = None if last else _tail_route(l, weights, lw, hp, o_mla, mem)
        hp = _mix_out(hp, o_mla, pooled, cact, lw)
        outs['ckv_p'].append(ckv.reshape(1, t_prompt, KV_LORA))
        outs['kpe_p'].append(kpe.reshape(1, t_prompt, ROPE_DIM))
        outs['pool_p'].append(xpool[-POOL_STATE:].reshape(1, POOL_STATE, POOL_WIDTH))
        outs['conv_p'].append(xconv[-CONV_STATE:].reshape(1, CONV_STATE, CONV_WIDTH))
        mk, mv = _mem_kv(mem, lw)
        outs['mk_p'].append(mk.reshape(1, MEM_TOKENS, MEM_HEADS, MEM_HEAD_DIM))
        outs['mv_p'].append(mv.reshape(1, MEM_TOKENS, MEM_HEADS, MEM_HEAD_DIM))
        hp = _cross_prompt(hp, mk, mv, lw)

        qlat, qpe, ckv_s, kpe_s, xpool_s, xconv_s = _mixer_in(hs, tab_s, lw, prompt=False)
        o_lat = _paged_attention(l, page_table, jnp.transpose(qlat, (1, 0, 2)), jnp.transpose(qpe, (1, 0, 2)),
                                 ckv_s.reshape(n_dec, 1, KV_LORA), kpe_s.reshape(n_dec, 1, ROPE_DIM),
                                 cache_ckv, cache_kpe_t)
        o_mla_s = _value_up(jnp.transpose(o_lat, (1, 0, 2)), lw['w_uv_h'])
        pooled_s, cact_s = _poolconv_sample(jnp.transpose(state_pool[l], (1, 0, 2)),
                                            jnp.transpose(state_conv[l], (1, 0, 2)), xpool_s, xconv_s, lw)
        hs = _mix_out(hs, o_mla_s, pooled_s, cact_s, lw)
        outs['ckv_s'].append(ckv_s.reshape(n_dec, 1, KV_LORA))
        outs['kpe_s'].append(kpe_s.reshape(n_dec, 1, ROPE_DIM))
        outs['pool_s'].append(jnp.concatenate([state_pool[l][:, 1:], xpool_s[:, None, :]], axis=1))
        outs['conv_s'].append(jnp.concatenate([state_conv[l][:, 1:], xconv_s[:, None, :]], axis=1))
        hs = _cross_sample(l, hs, mem_k_rows, mem_v_rows, lw)

        hp, hs = _moe(l, hp, hs, lw, weights, g_fin, final_norm=last, rec_tail=rec_tail)

    stack = lambda name: jnp.stack(outs[name])
    return (hp.reshape(1, t_prompt, D_MODEL), hs.reshape(n_dec, 1, D_MODEL),
            stack('ckv_p'), stack('kpe_p'), stack('pool_p'), stack('conv_p'), stack('mk_p'), stack('mv_p'),
            stack('ckv_s'), stack('kpe_s'), stack('pool_s'), stack('conv_s'))
```

```python
import functools

import jax
import jax.numpy as jnp
from jax import lax
from jax.experimental import pallas as pl
from jax.experimental.pallas import tpu as pltpu

F32 = jnp.float32
BF16 = jnp.bfloat16

D_MODEL = 2048
DEPTH = 2
PAST_LEN = 16384
PAGE_SIZE = 128
NOPE_DIM = 128
ROPE_DIM = 64
V_DIM = 128
MLA_HEADS = 8
Q_LORA = 768
KV_LORA = 256
ROPE_THETA = 10000.0
MLA_SCALE = (NOPE_DIM + ROPE_DIM) ** -0.5
LOG2_E = 1.4426950408889634
POOL_WIDTH = 512
CONV_WIDTH = 512
POOL_WINDOWS = (2, 4, 8, 16)
POOL_GROUP_WIDTH = 128
POOL_STATE = 15
CONV_K = 31
CONV_STATE = 30
MEM_TOKENS = 256
MEM_HEADS = 4
MEM_HEAD_DIM = 128
MEM_WIDTH = 512
MEM_SCALE = MEM_HEAD_DIM ** -0.5
N_GROUPS = 4
EXPERTS_PER_GROUP = 8
N_EXPERTS = 32
D_EXPERT = 512
EPS = 1e-6
NEG_INF = -1e30

_O_CQ = Q_LORA
_O_CKV = _O_CQ + KV_LORA
_O_POOL = _O_CKV + POOL_WIDTH
_O_GLU = _O_POOL + 2 * CONV_WIDTH
IN_COLS_R = _O_GLU + 2 * ROPE_DIM
Q_CHUNK = NOPE_DIM + 2 * ROPE_DIM

VMEM_LIMIT_V7X = 56 * 1024 * 1024
ROUTE_LANES = 128
PAGES_PER_STEP = 32
PAGED_CHUNK_PAGES = 8
MIXER_ROWS = 256
FLASH_BLOCK = 512
FLASH_HEADS = 4
MOE_TILE = 256


def _params(sem):
    return pltpu.CompilerParams(dimension_semantics=sem, vmem_limit_bytes=VMEM_LIMIT_V7X)


def _const(shape):
    return pl.BlockSpec(shape, lambda *_: (0,) * len(shape), pipeline_mode=pl.Buffered(1))


def _rms(x, g):
    ms = jnp.mean(x * x, axis=-1, keepdims=True)
    return x * lax.rsqrt(ms + EPS) * g


def _precision(a):
    return lax.Precision.HIGHEST if a.dtype == F32 else None


def _dot(a, b):
    return jnp.dot(a, b, preferred_element_type=F32, precision=_precision(a))


def _dot_nt(a, b):
    return lax.dot_general(a, b, (((1,), (1,)), ((), ())), preferred_element_type=F32, precision=_precision(a))


def _mixer_common(h_ref, tab_ref, gmix_ref, win_ref, gq_ref, wq_ref, gkv_ref,
                  ckv_ref, kpe_ref, xpool_ref, xconv_ref):
    xn = _rms(h_ref[...], gmix_ref[...]).astype(BF16)
    u = _dot(xn, win_ref[...])
    cq = _rms(u[:, :_O_CQ], gq_ref[...]).astype(BF16)
    ckv = _rms(u[:, _O_CQ:_O_CKV], gkv_ref[...])
    ckv_ref[...] = ckv
    xpool_ref[...] = u[:, _O_CKV:_O_POOL]
    xconv_ref[...] = u[:, _O_POOL:_O_POOL + CONV_WIDTH] * jax.nn.sigmoid(u[:, _O_POOL + CONV_WIDTH:_O_GLU])
    tab = tab_ref[...]
    kp = u[:, _O_GLU:] * tab
    kp2 = kp + pltpu.roll(kp, ROPE_DIM, 1)
    kpe_ref[...] = kp2[:, :ROPE_DIM]
    qraw = _dot(cq, wq_ref[...])
    return ckv, kp2, qraw, tab


def _mixer_prompt_kernel(h_ref, tab_ref, gmix_ref, win_ref, gq_ref, wq_ref, gkv_ref, wuk_ref, wuv_ref,
                         q_ref, k_ref, v_ref, ckv_ref, kpe_ref, xpool_ref, xconv_ref):
    ckv, kp2, qraw, tab = _mixer_common(h_ref, tab_ref, gmix_ref, win_ref, gq_ref, wq_ref, gkv_ref,
                                        ckv_ref, kpe_ref, xpool_ref, xconv_ref)
    ckv_b = ckv.astype(BF16)
    kn = _dot(ckv_b, wuk_ref[...])
    vv = _dot(ckv_b, wuv_ref[...])
    kp2b = kp2.astype(BF16)
    for hh in range(MLA_HEADS):
        c0 = Q_CHUNK * hh
        q_ref[hh, :, :NOPE_DIM] = qraw[:, c0:c0 + NOPE_DIM].astype(BF16)
        q_ref[hh, :, NOPE_DIM:] = (qraw[:, c0 + NOPE_DIM:c0 + Q_CHUNK] * tab).astype(BF16)
        k_ref[hh, :, :NOPE_DIM] = kn[:, NOPE_DIM * hh:NOPE_DIM * (hh + 1)].astype(BF16)
        k_ref[hh, :, NOPE_DIM:] = kp2b
        v_ref[hh, 0] = vv[:, V_DIM * hh:V_DIM * (hh + 1)].T.astype(BF16)


def _mixer_sample_kernel(h_ref, tab_ref, gmix_ref, win_ref, gq_ref, wq_ref, gkv_ref, wukt_ref,
                         qlat_ref, qpe_ref, ckv_ref, kpe_ref, xpool_ref, xconv_ref):
    _, _, qraw, tab = _mixer_common(h_ref, tab_ref, gmix_ref, win_ref, gq_ref, wq_ref, gkv_ref,
                                    ckv_ref, kpe_ref, xpool_ref, xconv_ref)
    for hh in range(MLA_HEADS):
        c0 = Q_CHUNK * hh
        qn = qraw[:, c0:c0 + NOPE_DIM].astype(BF16)
        qlat_ref[hh] = _dot(qn, wukt_ref[hh]).astype(BF16)
        qp = qraw[:, c0 + NOPE_DIM:c0 + Q_CHUNK] * tab
        qp2 = qp + pltpu.roll(qp, ROPE_DIM, 1)
        qpe_ref[hh] = qp2[:, :ROPE_DIM].astype(BF16)


def _mixer_in(h, tab, lw, *, prompt):
    t = h.shape[0]
    tm = MIXER_ROWS if prompt else t
    row = lambda w: pl.BlockSpec((tm, w), lambda i: (i, 0))
    head = lambda w: pl.BlockSpec((MLA_HEADS, tm, w), lambda i: (0, i, 0))
    in_specs = [row(D_MODEL), row(2 * ROPE_DIM), _const((1, D_MODEL)), _const((D_MODEL, IN_COLS_R)),
                _const((1, Q_LORA)), _const((Q_LORA, MLA_HEADS * Q_CHUNK)), _const((1, KV_LORA))]
    args = [h, tab, lw['g_mix'], lw['w_in'], lw['g_q_lat'], lw['w_q'], lw['g_kv_lat']]
    tail_specs = [row(KV_LORA), row(ROPE_DIM), row(POOL_WIDTH), row(CONV_WIDTH)]
    tail_shapes = [jax.ShapeDtypeStruct((t, KV_LORA), F32), jax.ShapeDtypeStruct((t, ROPE_DIM), F32),
                   jax.ShapeDtypeStruct((t, POOL_WIDTH), F32), jax.ShapeDtypeStruct((t, CONV_WIDTH), F32)]
    if prompt:
        kern = _mixer_prompt_kernel
        in_specs += [_const((KV_LORA, MLA_HEADS * NOPE_DIM)), _const((KV_LORA, MLA_HEADS * V_DIM))]
        args += [lw['w_uk'], lw['w_uv']]
        vt_spec = pl.BlockSpec((MLA_HEADS, 1, V_DIM, tm), lambda i: (0, i, 0, 0))
        out_specs = [head(2 * NOPE_DIM), head(2 * NOPE_DIM), vt_spec] + tail_specs
        out_shape = [jax.ShapeDtypeStruct((MLA_HEADS, t, 2 * NOPE_DIM), BF16),
                     jax.ShapeDtypeStruct((MLA_HEADS, t, 2 * NOPE_DIM), BF16),
                     jax.ShapeDtypeStruct((MLA_HEADS, t // tm, V_DIM, tm), BF16)] + tail_shapes
    else:
        kern = _mixer_sample_kernel
        in_specs += [_const((MLA_HEADS, NOPE_DIM, KV_LORA))]
        args += [lw['w_uk_t']]
        out_specs = [head(KV_LORA), head(ROPE_DIM)] + tail_specs
        out_shape = [jax.ShapeDtypeStruct((MLA_HEADS, t, KV_LORA), BF16),
                     jax.ShapeDtypeStruct((MLA_HEADS, t, ROPE_DIM), BF16)] + tail_shapes
    return pl.pallas_call(kern, grid=(t // tm,), in_specs=in_specs, out_specs=out_specs,
                          out_shape=out_shape, compiler_params=_params(("parallel",)))(*args)


def _flash_kernel(q_ref, k_ref, vt_ref, o_ref, m_scr, l_scr, acc_scr):
    blk = FLASH_BLOCK
    sub = blk // MIXER_ROWS
    qi = pl.program_id(1)
    m_scr[...] = jnp.full_like(m_scr, -jnp.inf)
    l_scr[...] = jnp.zeros_like(l_scr)
    acc_scr[...] = jnp.zeros_like(acc_scr)

    def step(kv, diagonal):
        start = pl.multiple_of(kv * blk, blk)
        for hh in range(FLASH_HEADS):
            k = k_ref[hh, pl.ds(start, blk), :]
            st = _dot_nt(k, q_ref[hh])
            if diagonal:
                key = lax.broadcasted_iota(jnp.int32, (blk, blk), 0)
                qry = lax.broadcasted_iota(jnp.int32, (blk, blk), 1)
                st = jnp.where(key <= qry, st, NEG_INF)
            m_prev = m_scr[hh]
            m_new = jnp.maximum(m_prev, jnp.max(st, axis=0, keepdims=True))
            p = jnp.exp2(st - m_new)
            alpha = jnp.exp2(m_prev - m_new)
            l_scr[hh] = alpha * l_scr[hh] + jnp.sum(p, axis=0, keepdims=True)
            pb = p.astype(BF16)
            pv = _dot(vt_ref[hh, kv * sub], pb[:MIXER_ROWS])
            for c in range(1, sub):
                pv = pv + _dot(vt_ref[hh, kv * sub + c], pb[c * MIXER_ROWS:(c + 1) * MIXER_ROWS])
            acc_scr[hh] = alpha * acc_scr[hh] + pv
            m_scr[hh] = m_new

    def body(kv, carry):
        step(kv, False)
        return carry

    lax.fori_loop(0, qi, body, 0)
    step(qi, True)
    for hh in range(FLASH_HEADS):
        o_ref[:, V_DIM * hh:V_DIM * (hh + 1)] = (acc_scr[hh] / l_scr[hh]).T.astype(o_ref.dtype)


def _flash_attention(q, k, vt):
    nh, t, dk = q.shape
    blk, g = FLASH_BLOCK, FLASH_HEADS
    return pl.pallas_call(
        _flash_kernel,
        grid=(nh // g, t // blk),
        in_specs=[pl.BlockSpec((g, blk, dk), lambda h, i: (h, i, 0)),
                  pl.BlockSpec((g, t, dk), lambda h, i: (h, 0, 0), pipeline_mode=pl.Buffered(1)),
                  pl.BlockSpec((g, t // MIXER_ROWS, V_DIM, MIXER_ROWS), lambda h, i: (h, 0, 0, 0),
                               pipeline_mode=pl.Buffered(1))],
        out_specs=pl.BlockSpec((blk, g * V_DIM), lambda h, i: (i, h)),
        out_shape=jax.ShapeDtypeStruct((t, nh * V_DIM), F32),
        scratch_shapes=[pltpu.VMEM((g, 1, blk), F32), pltpu.VMEM((g, 1, blk), F32), pltpu.VMEM((g, V_DIM, blk), F32)],
        compiler_params=_params(("parallel", "parallel")))(q, k, vt)


def _page_copies(pt_ref, ckv_hbm, kpet_hbm, ckv_stage, kpet_stage, sems, layer, seq, step, slot):
    copies = []
    for i in range(PAGES_PER_STEP):
        page = pt_ref[seq, step * PAGES_PER_STEP + i]
        tok = pl.ds(i * PAGE_SIZE, PAGE_SIZE)
        copies.append(pltpu.make_async_copy(ckv_hbm.at[layer, page], ckv_stage.at[slot, tok, :], sems.at[0, slot]))
        copies.append(pltpu.make_async_copy(kpet_hbm.at[layer, page], kpet_stage.at[slot, :, tok], sems.at[1, slot]))
    return copies


def _paged_kernel(pt_ref, qlat_ref, qpe_ref, ckvn_ref, kpen_ref, ckv_hbm, kpet_hbm, o_ref,
                  m_scr, l_scr, acc_scr, ckv_stage, kpet_stage, sems, *, layer, n_seqs, n_steps):
    b = pl.program_id(0)
    j = pl.program_id(1)
    n = b * n_steps + j
    slot = lax.rem(n, 2)
    copies = functools.partial(_page_copies, pt_ref, ckv_hbm, kpet_hbm, ckv_stage, kpet_stage, sems, layer)

    @pl.when(n == 0)
    def _():
        for c in copies(0, 0, 0):
            c.start()

    @pl.when(n + 1 < n_seqs * n_steps)
    def _():
        wrap = j == n_steps - 1
        for c in copies(jnp.where(wrap, b + 1, b), jnp.where(wrap, 0, j + 1), 1 - slot):
            c.start()

    ql = qlat_ref[0]
    qp = qpe_ref[0]

    @pl.when(j == 0)
    def _():
        cn = ckvn_ref[0]
        kn = kpen_ref[0]
        s_self = (jnp.sum(ql.astype(F32) * cn, axis=1, keepdims=True)
                  + jnp.sum(qp.astype(F32) * kn, axis=1, keepdims=True))
        m_scr[...] = s_self
        l_scr[...] = jnp.ones_like(l_scr)
        acc_scr[...] = jnp.broadcast_to(cn, acc_scr.shape)

    for c in copies(b, j, slot):
        c.wait()

    chunk = PAGED_CHUNK_PAGES * PAGE_SIZE
    kcs, ss = [], []
    for c in range(PAGES_PER_STEP // PAGED_CHUNK_PAGES):
        kc = ckv_stage[slot, c * chunk:(c + 1) * chunk, :].astype(BF16)
        kp = kpet_stage[slot, :, c * chunk:(c + 1) * chunk].astype(BF16)
        kcs.append(kc)
        ss.append(_dot_nt(ql, kc) + _dot(qp, kp))
    s = jnp.concatenate(ss, axis=1)
    m_prev = m_scr[...]
    m_new = jnp.maximum(m_prev, jnp.max(s, axis=1, keepdims=True))
    p = jnp.exp2(s - m_new)
    alpha = jnp.exp2(m_prev - m_new)
    l_scr[...] = alpha * l_scr[...] + jnp.sum(p, axis=1, keepdims=True)
    pb = p.astype(BF16)
    pv = _dot(pb[:, :chunk], kcs[0])
    for c in range(1, len(kcs)):
        pv = pv + _dot(pb[:, c * chunk:(c + 1) * chunk], kcs[c])
    acc_scr[...] = alpha * acc_scr[...] + pv
    m_scr[...] = m_new

    @pl.when(j == n_steps - 1)
    def _():
        o_ref[0] = acc_scr[...] / l_scr[...]


def _paged_attention(layer, page_table, qlat, qpe, ckv_new, kpe_new, cache_ckv, cache_kpe_t):
    s, n_pages = page_table.shape
    n_steps = n_pages // PAGES_PER_STEP
    step_tokens = PAGES_PER_STEP * PAGE_SIZE
    per_seq = lambda r, w: pl.BlockSpec((1, r, w), lambda b, j, pt: (b, 0, 0))
    hbm = pl.BlockSpec(memory_space=pl.ANY)
    grid_spec = pltpu.PrefetchScalarGridSpec(
        num_scalar_prefetch=1, grid=(s, n_steps),
        in_specs=[per_seq(MLA_HEADS, KV_LORA), per_seq(MLA_HEADS, ROPE_DIM), per_seq(1, KV_LORA), per_seq(1, ROPE_DIM),
                  hbm, hbm],
        out_specs=per_seq(MLA_HEADS, KV_LORA),
        scratch_shapes=[pltpu.VMEM((MLA_HEADS, 1), F32), pltpu.VMEM((MLA_HEADS, 1), F32),
                        pltpu.VMEM((MLA_HEADS, KV_LORA), F32),
                        pltpu.VMEM((2, step_tokens, KV_LORA), F32), pltpu.VMEM((2, ROPE_DIM, step_tokens), F32),
                        pltpu.SemaphoreType.DMA((2, 2))])
    return pl.pallas_call(
        functools.partial(_paged_kernel, layer=layer, n_seqs=s, n_steps=n_steps), grid_spec=grid_spec,
        out_shape=jax.ShapeDtypeStruct((s, MLA_HEADS, KV_LORA), F32),
        compiler_params=_params(("arbitrary", "arbitrary")))(
            page_table, qlat, qpe, ckv_new, kpe_new, cache_ckv, cache_kpe_t)


def _value_up_kernel(olat_ref, wuv_ref, o_ref):
    for hh in range(MLA_HEADS):
        o_ref[:, V_DIM * hh:V_DIM * (hh + 1)] = _dot(olat_ref[hh].astype(BF16), wuv_ref[hh]).astype(o_ref.dtype)


def _value_up(olat_h, w_uv_h):
    s = olat_h.shape[1]
    return pl.pallas_call(_value_up_kernel, out_shape=jax.ShapeDtypeStruct((s, MLA_HEADS * V_DIM), BF16))(olat_h, w_uv_h)


def _ln_silu(y, g, b):
    mu = jnp.mean(y, axis=-1, keepdims=True)
    d = y - mu
    var = jnp.mean(d * d, axis=-1, keepdims=True)
    z = d * lax.rsqrt(var + EPS) * g + b
    return z * jax.nn.sigmoid(z)


_POOL_PAD = 16
_CONV_PAD = 32


def _poolconv_prompt_kernel(xp_ref, xc_ref, wdw_ref, bdw_ref, gln_ref, bln_ref, pooled_ref, cact_ref,
                            extp, extc, *, tm, pos0):
    i = pl.program_id(0)

    @pl.when(i == 0)
    def _():
        extp[0:_POOL_PAD, :] = jnp.zeros((_POOL_PAD, POOL_WIDTH), F32)
        extc[0:_CONV_PAD, :] = jnp.zeros((_CONV_PAD, CONV_WIDTH), F32)

    extp[_POOL_PAD:, :] = xp_ref[...]
    extc[_CONV_PAD:, :] = xc_ref[...]
    pos = pos0 + i * tm + lax.broadcasted_iota(jnp.int32, (tm, 1), 0)
    for gi, w in enumerate(POOL_WINDOWS):
        c0 = gi * POOL_GROUP_WIDTH
        x_new = extp[_POOL_PAD:, c0:c0 + POOL_GROUP_WIDTH]
        acc = x_new
        for k in range(1, w):
            acc = acc + extp[_POOL_PAD - k:_POOL_PAD - k + tm, c0:c0 + POOL_GROUP_WIDTH]
        cnt = jnp.minimum(pos + 1, w).astype(F32)
        pooled_ref[:, c0:c0 + POOL_GROUP_WIDTH] = (acc / cnt - x_new).astype(pooled_ref.dtype)
    base = _CONV_PAD - CONV_STATE
    acc = extc[base:base + tm, :] * wdw_ref[0:1, :]
    for k in range(1, CONV_K):
        acc = acc + extc[base + k:base + k + tm, :] * wdw_ref[k:k + 1, :]
    cact_ref[...] = _ln_silu(acc + bdw_ref[...], gln_ref[...], bln_ref[...]).astype(cact_ref.dtype)
    extp[0:_POOL_PAD, :] = extp[tm:tm + _POOL_PAD, :]
    extc[0:_CONV_PAD, :] = extc[tm:tm + _CONV_PAD, :]


def _poolconv_prompt(xpool, xconv, lw, *, pos0=0, out_dtype=BF16):
    t = xpool.shape[0]
    tm = min(t, 512)
    row = pl.BlockSpec((tm, POOL_WIDTH), lambda i: (i, 0))
    vec = _const((1, CONV_WIDTH))
    return pl.pallas_call(
        functools.partial(_poolconv_prompt_kernel, tm=tm, pos0=pos0), grid=(t // tm,),
        in_specs=[row, row, _const((CONV_K, CONV_WIDTH)), vec, vec, vec],
        out_specs=[row, row],
        out_shape=[jax.ShapeDtypeStruct((t, POOL_WIDTH), out_dtype), jax.ShapeDtypeStruct((t, CONV_WIDTH), out_dtype)],
        scratch_shapes=[pltpu.VMEM((tm + _POOL_PAD, POOL_WIDTH), F32), pltpu.VMEM((tm + _CONV_PAD, CONV_WIDTH), F32)],
        compiler_params=_params(("arbitrary",)))(
            xpool, xconv, lw['w_conv_dw'], lw['b_conv_dw'], lw['g_conv_ln'], lw['b_conv_ln'])


def _poolconv_sample_kernel(sp_ref, sc_ref, xp_ref, xc_ref, wdw_ref, bdw_ref, gln_ref, bln_ref,
                            pooled_ref, cact_ref):
    for gi, w in enumerate(POOL_WINDOWS):
        c0 = gi * POOL_GROUP_WIDTH
        x_new = xp_ref[:, c0:c0 + POOL_GROUP_WIDTH]
        acc = x_new
        for k in range(1, w):
            acc = acc + sp_ref[POOL_STATE - k, :, c0:c0 + POOL_GROUP_WIDTH]
        cnt = float(min(PAST_LEN + 1, w))
        pooled_ref[:, c0:c0 + POOL_GROUP_WIDTH] = (acc / cnt - x_new).astype(pooled_ref.dtype)
    acc = xc_ref[...] * wdw_ref[CONV_K - 1:CONV_K, :]
    for k in range(CONV_STATE):
        acc = acc + sc_ref[k] * wdw_ref[k:k + 1, :]
    cact_ref[...] = _ln_silu(acc + bdw_ref[...], gln_ref[...], bln_ref[...]).astype(cact_ref.dtype)


def _poolconv_sample(state_pool_t, state_conv_t, xpool, xconv, lw):
    s = xpool.shape[0]
    return pl.pallas_call(
        _poolconv_sample_kernel,
        out_shape=[jax.ShapeDtypeStruct((s, POOL_WIDTH), BF16), jax.ShapeDtypeStruct((s, CONV_WIDTH), BF16)])(
            state_pool_t, state_conv_t, xpool, xconv,
            lw['w_conv_dw'], lw['b_conv_dw'], lw['g_conv_ln'], lw['b_conv_ln'])


def _mix_out_kernel(h_ref, omla_ref, pooled_ref, cact_ref, wpool_ref, pscale_ref, wpw_ref, wout_ref, o_ref):
    cdt = wout_ref.dtype
    acc = h_ref[...] + _dot(omla_ref[...].astype(cdt), wout_ref[0:1024, :])
    pooled = pooled_ref[...]
    for gi in range(len(POOL_WINDOWS)):
        c0 = gi * POOL_GROUP_WIDTH
        og = _dot(pooled[:, c0:c0 + POOL_GROUP_WIDTH], wpool_ref[gi]) * pscale_ref[:, c0:c0 + POOL_GROUP_WIDTH]
        acc = acc + _dot(og.astype(cdt), wout_ref[1024 + c0:1024 + c0 + POOL_GROUP_WIDTH, :])
    oc = _dot(cact_ref[...], wpw_ref[...]).astype(cdt)
    o_ref[...] = acc + _dot(oc, wout_ref[1536:2048, :])


def _mix_out(h, omla, pooled, cact, lw):
    t = h.shape[0]
    tm = min(t, 512)
    row = lambda w: pl.BlockSpec((tm, w), lambda i: (i, 0))
    return pl.pallas_call(
        _mix_out_kernel, grid=(t // tm,),
        in_specs=[row(D_MODEL), row(1024), row(POOL_WIDTH), row(CONV_WIDTH),
                  _const((len(POOL_WINDOWS), POOL_GROUP_WIDTH, POOL_GROUP_WIDTH)), _const((1, POOL_WIDTH)),
                  _const((CONV_WIDTH, CONV_WIDTH)), _const((D_MODEL, D_MODEL))],
        out_specs=row(D_MODEL), out_shape=jax.ShapeDtypeStruct((t, D_MODEL), F32),
        compiler_params=_params(("parallel",)))(
            h, omla, pooled, cact, lw['w_pool'], lw['pool_scale'], lw['w_conv_pw'], lw['w_out'])


def _mem_kv_kernel(mem_ref, g_ref, wk_ref, wv_ref, k_ref, v_ref):
    m = _rms(mem_ref[...], g_ref[...]).astype(wk_ref.dtype)
    k_ref[...] = _dot(m, wk_ref[...])
    v_ref[...] = _dot(m, wv_ref[...])


def _mem_kv(mem, lw):
    n = mem.shape[0]
    shp = jax.ShapeDtypeStruct((n, MEM_WIDTH), F32)
    return pl.pallas_call(_mem_kv_kernel, out_shape=[shp, shp],
                          compiler_params=_params(None))(mem, lw['g_mem_tok'], lw['w_mk'], lw['w_mv'])


def _cross_prompt_kernel(h_ref, g_ref, wq_ref, mk_ref, mv_ref, wo_ref, o_ref):
    cdt = wq_ref.dtype
    h = h_ref[...]
    q = _dot(_rms(h, g_ref[...]).astype(cdt), wq_ref[...]).astype(cdt)
    mk = mk_ref[...].astype(cdt)
    mv = mv_ref[...].astype(cdt)
    acc = h
    for hh in range(MEM_HEADS):
        c = slice(MEM_HEAD_DIM * hh, MEM_HEAD_DIM * (hh + 1))
        s = _dot_nt(q[:, c], mk[:, c])
        p = jnp.exp(s - jnp.max(s, axis=1, keepdims=True))
        p = p / jnp.sum(p, axis=1, keepdims=True)
        oh = _dot(p.astype(cdt), mv[:, c]).astype(cdt)
        acc = acc + _dot(oh, wo_ref[c, :])
    o_ref[...] = acc


def _cross_prompt(h, mk, mv, lw):
    t = h.shape[0]
    tm = min(t, 512)
    row = pl.BlockSpec((tm, D_MODEL), lambda i: (i, 0))
    return pl.pallas_call(
        _cross_prompt_kernel, grid=(t // tm,),
        in_specs=[row, _const((1, D_MODEL)), _const((D_MODEL, MEM_WIDTH)), _const((MEM_TOKENS, MEM_WIDTH)),
                  _const((MEM_TOKENS, MEM_WIDTH)), _const((MEM_WIDTH, D_MODEL))],
        out_specs=row, out_shape=jax.ShapeDtypeStruct((t, D_MODEL), F32),
        compiler_params=_params(("parallel",)))(h, lw['g_mem_x'], lw['w_mq'], mk, mv, lw['w_mo'])


_CROSS_SEQS = 8


def _cross_sample_kernel(h_ref, g_ref, wq_ref, k_ref, v_ref, wo_ref, o_ref):
    h = h_ref[...]
    q = _dot(_rms(h, g_ref[...]).astype(BF16), wq_ref[...])
    rows = 8
    n_rows = MEM_TOKENS * MEM_HEADS
    col_head = jnp.bitwise_and(lax.broadcasted_iota(jnp.int32, (rows, n_rows), 1), MEM_HEADS - 1)
    own = col_head == lax.broadcasted_iota(jnp.int32, (rows, n_rows), 0)
    pad = jnp.zeros((rows - MEM_HEADS, MEM_HEAD_DIM), F32)
    outs = []
    for b in range(_CROSS_SEQS):
        qb = jnp.concatenate([q[b:b + 1, MEM_HEAD_DIM * hh:MEM_HEAD_DIM * (hh + 1)] for hh in range(MEM_HEADS)]
                             + [pad], axis=0).astype(BF16)
        kb = k_ref[b].astype(BF16)
        vb = v_ref[b].astype(BF16)
        s = jnp.where(own, _dot_nt(qb, kb), NEG_INF)
        p = jnp.exp(s - jnp.max(s, axis=1, keepdims=True))
        p = p / jnp.sum(p, axis=1, keepdims=True)
        ob = _dot(p.astype(BF16), vb)
        outs.append(jnp.concatenate([ob[hh:hh + 1, :] for hh in range(MEM_HEADS)], axis=1))
    o = jnp.concatenate(outs, axis=0).astype(BF16)
    o_ref[...] = h + _dot(o, wo_ref[...])


def _cross_sample(layer, h, mem_k, mem_v, lw):
    s = h.shape[0]
    row = pl.BlockSpec((_CROSS_SEQS, D_MODEL), lambda i: (i, 0))
    kv = pl.BlockSpec((None, _CROSS_SEQS, MEM_TOKENS * MEM_HEADS, MEM_HEAD_DIM), lambda i: (layer, i, 0, 0))
    return pl.pallas_call(
        _cross_sample_kernel, grid=(s // _CROSS_SEQS,),
        in_specs=[row, _const((1, D_MODEL)), _const((D_MODEL, MEM_WIDTH)), kv, kv, _const((MEM_WIDTH, D_MODEL))],
        out_specs=row, out_shape=jax.ShapeDtypeStruct((s, D_MODEL), F32),
        compiler_params=_params(("parallel",)))(h, lw['g_mem_x'], lw['w_mq'], mem_k, mem_v, lw['w_mo'])


def _route_kernel(h_ref, g_ref, wr_ref, br_ref, xn_ref, rec_ref):
    xn = _rms(h_ref[...], g_ref[...])
    xn_ref[...] = xn
    logits = jnp.dot(xn, wr_ref[...], preferred_element_type=F32, precision=lax.Precision.HIGHEST) + br_ref[...]
    col = lax.broadcasted_iota(jnp.int32, logits.shape, 1)
    big = jnp.int32(ROUTE_LANES)
    is_group = col < N_GROUPS
    lg = jnp.where(is_group, logits, NEG_INF)
    g_max = jnp.max(lg, axis=1, keepdims=True)
    g_sel = jnp.min(jnp.where(lg == g_max, col, big), axis=1, keepdims=True)
    p_sel = 1.0 / jnp.sum(jnp.where(is_group, jnp.exp(lg - g_max), 0.0), axis=1, keepdims=True)
    e_col = col - N_GROUPS
    e_group = jnp.right_shift(e_col, EXPERTS_PER_GROUP.bit_length() - 1)
    in_group = (e_col >= 0) & (e_col < N_EXPERTS) & (e_group == g_sel)
    le = jnp.where(in_group, logits, NEG_INF)
    v1 = jnp.max(le, axis=1, keepdims=True)
    i1 = jnp.min(jnp.where(le == v1, col, big), axis=1, keepdims=True)
    le2 = jnp.where(col == i1, NEG_INF, le)
    v2 = jnp.max(le2, axis=1, keepdims=True)
    i2 = jnp.min(jnp.where(le2 == v2, col, big), axis=1, keepdims=True)
    e2 = jnp.exp(v2 - v1)
    w1 = p_sel / (1.0 + e2)
    w2 = p_sel * e2 / (1.0 + e2)
    rec = jnp.where(col == 0, (i1 - N_GROUPS).astype(F32),
                    jnp.where(col == 1, (i2 - N_GROUPS).astype(F32),
                              jnp.where(col == 2, w1, jnp.where(col == 3, w2, 0.0))))
    rec_ref[...] = rec


def _route(h, lw):
    t = h.shape[0]
    tm = min(t, 256)
    row = lambda w: pl.BlockSpec((tm, w), lambda i: (i, 0))
    return pl.pallas_call(
        _route_kernel, grid=(t // tm,),
        in_specs=[row(D_MODEL), _const((1, D_MODEL)), _const((D_MODEL, ROUTE_LANES)), _const((1, ROUTE_LANES))],
        out_specs=[row(D_MODEL), row(ROUTE_LANES)],
        out_shape=[jax.ShapeDtypeStruct((t, D_MODEL), F32), jax.ShapeDtypeStruct((t, ROUTE_LANES), F32)],
        compiler_params=_params(("parallel",)))(h, lw['g_ffn'], lw['w_route'], lw['b_route'])


def _tile_row_copy(tok_ref, x_hbm, xbuf, sem, tile, slot, r):
    tok = tok_ref[tile * MOE_TILE + r]
    return pltpu.make_async_copy(x_hbm.at[pl.ds(tok, 1)], xbuf.at[slot, pl.ds(r, 1)], sem.at[slot])


def _experts_kernel(te_ref, nt_ref, tok_ref, x_hbm, wg_ref, wu_ref, wd_ref, o_ref, wg_b, wu_b, wd_b, xbuf, sem):
    i = pl.program_id(0)
    n_used = nt_ref[0]
    used = i < n_used
    slot = lax.rem(i, 2)
    new_expert = (i == 0) | (te_ref[i] != te_ref[jnp.maximum(i - 1, 0)])
    row_copy = functools.partial(_tile_row_copy, tok_ref, x_hbm, xbuf, sem)

    def start_tile(tile, sl):
        def body(r, carry):
            row_copy(tile, sl, r).start()
            return carry
        lax.fori_loop(0, MOE_TILE, body, 0, unroll=8)

    @pl.when((i == 0) & used)
    def _():
        start_tile(0, 0)

    @pl.when(i + 1 < n_used)
    def _():
        start_tile(i + 1, 1 - slot)

    @pl.when(used & new_expert)
    def _():
        wg_b[...] = wg_ref[...].astype(BF16)
        wu_b[...] = wu_ref[...].astype(BF16)
        wd_b[...] = wd_ref[...].astype(BF16)

    @pl.when(used)
    def _():
        def body(r, carry):
            row_copy(i, slot, r).wait()
            return carry
        lax.fori_loop(0, MOE_TILE, body, 0, unroll=8)
        x = xbuf[slot].astype(BF16)
        hg = _dot(x, wg_b[...])
        hu = _dot(x, wu_b[...])
        act = (hg * jax.nn.sigmoid(hg) * hu).astype(BF16)
        o_ref[...] = _dot(act, wd_b[...])

    @pl.when(jnp.logical_not(used))
    def _():
        o_ref[...] = jnp.zeros_like(o_ref)


def _experts(layer, tile_expert, n_tiles_used, slot_token, x_rows, w_gate, w_up, w_down):
    r = slot_token.shape[0]
    n_tiles = r // MOE_TILE
    w_spec = lambda a, b: pl.BlockSpec((None, None, a, b), lambda i, te, nt, tok: (layer, te[i], 0, 0))
    grid_spec = pltpu.PrefetchScalarGridSpec(
        num_scalar_prefetch=3, grid=(n_tiles,),
        in_specs=[pl.BlockSpec(memory_space=pl.ANY),
                  w_spec(D_MODEL, D_EXPERT), w_spec(D_MODEL, D_EXPERT), w_spec(D_EXPERT, D_MODEL)],
        out_specs=pl.BlockSpec((MOE_TILE, D_MODEL), lambda i, te, nt, tok: (i, 0)),
        scratch_shapes=[pltpu.VMEM((D_MODEL, D_EXPERT), BF16), pltpu.VMEM((D_MODEL, D_EXPERT), BF16),
                        pltpu.VMEM((D_EXPERT, D_MODEL), BF16),
                        pltpu.VMEM((2, MOE_TILE, D_MODEL), F32), pltpu.SemaphoreType.DMA((2,))])
    return pl.pallas_call(
        _experts_kernel, grid_spec=grid_spec, out_shape=jax.ShapeDtypeStruct((r, D_MODEL), F32),
        compiler_params=_params(("arbitrary",)))(
            tile_expert, n_tiles_used, slot_token, x_rows, w_gate, w_up, w_down)


def _combine_kernel(h_ref, a_ref, b_ref, rec_ref, gf_ref, o_ref, *, final_norm):
    rec = rec_ref[...]
    y = h_ref[...] + rec[:, 2:3] * a_ref[...] + rec[:, 3:4] * b_ref[...]
    o_ref[...] = _rms(y, gf_ref[...]) if final_norm else y


def _combine(h, ya, yb, rec, g_final, *, final_norm):
    t = h.shape[0]
    tm = min(t, 512)
    row = lambda w: pl.BlockSpec((tm, w), lambda i: (i, 0))
    return pl.pallas_call(
        functools.partial(_combine_kernel, final_norm=final_norm), grid=(t // tm,),
        in_specs=[row(D_MODEL), row(D_MODEL), row(D_MODEL), row(ROUTE_LANES), _const((1, D_MODEL))],
        out_specs=row(D_MODEL), out_shape=jax.ShapeDtypeStruct((t, D_MODEL), F32),
        compiler_params=_params(("parallel",)))(h, ya, yb, rec, g_final)


def _dispatch_plan(expert_ids):
    n_pairs = expert_ids.shape[0]
    n_rows = -(-(n_pairs + N_EXPERTS * (MOE_TILE - 1)) // MOE_TILE) * MOE_TILE
    one_hot = (expert_ids[:, None] == jnp.arange(N_EXPERTS, dtype=jnp.int32)[None, :]).astype(jnp.int32)
    running = jnp.cumsum(one_hot, axis=0)
    rank = jnp.sum(running * one_hot, axis=1) - 1
    counts = running[-1]
    padded = -(-counts // MOE_TILE) * MOE_TILE
    ends = jnp.cumsum(padded)
    starts = ends - padded
    pair_slot = jnp.sum(one_hot * starts[None, :], axis=1) + rank
    slot_pair = jnp.zeros((n_rows,), jnp.int32).at[pair_slot].set(jnp.arange(n_pairs, dtype=jnp.int32),
                                                                   unique_indices=True)
    tile_start = jnp.arange(n_rows // MOE_TILE, dtype=jnp.int32) * MOE_TILE
    tile_expert = jnp.sum((tile_start[:, None] >= ends[None, :]).astype(jnp.int32), axis=1)
    tile_expert = jnp.minimum(tile_expert, N_EXPERTS - 1)
    n_tiles_used = (ends[-1] // MOE_TILE).astype(jnp.int32).reshape(1)
    return pair_slot, slot_pair, tile_expert, n_tiles_used


def _moe(layer, hp, hs, lw, w, g_final, *, final_norm, rec_tail=None):
    xn_p, rec_p = _route(hp, lw)
    xn_s, rec_s = _route(hs, lw)
    tp = hp.shape[0]
    if rec_tail is not None:
        rec_p = jnp.concatenate([rec_p[:tp - rec_tail.shape[0]], rec_tail], axis=0)
    rec = jnp.concatenate([rec_p, rec_s], axis=0)
    expert_ids = rec[:, :2].astype(jnp.int32).reshape(-1)
    pair_slot, slot_pair, tile_expert, n_tiles_used = _dispatch_plan(expert_ids)
    xn = jnp.concatenate([xn_p, xn_s], axis=0)
    y_sorted = _experts(layer, tile_expert, n_tiles_used, slot_pair // 2, xn, w['w_gate'], w['w_up'], w['w_down'])
    slots = pair_slot.reshape(-1, 2)
    rows = lambda idx: jnp.take(y_sorted, idx, axis=0, mode='clip')
    out_p = _combine(hp, rows(slots[:tp, 0]), rows(slots[:tp, 1]), rec_p, g_final, final_norm=final_norm)
    out_s = _combine(hs, rows(slots[tp:, 0]), rows(slots[tp:, 1]), rec_s, g_final, final_norm=final_norm)
    return out_p, out_s


_TAIL_ROWS = 32
_TAIL_CTX = 64


def _tail_in_kernel(h_ref, gmix_ref, w_ref, xpool_ref, xconv_ref):
    u = _dot(_rms(h_ref[...], gmix_ref[...]), w_ref[...])
    xpool_ref[...] = u[:, :POOL_WIDTH]
    xconv_ref[...] = u[:, POOL_WIDTH:POOL_WIDTH + CONV_WIDTH] * jax.nn.sigmoid(u[:, POOL_WIDTH + CONV_WIDTH:])


def _tail_route(l, w, lw, h_in, o_mla, mem):
    t = h_in.shape[0]
    h_ctx = h_in[t - _TAIL_CTX:]
    w_pool_glu = w['w_in'][l][:, Q_LORA + KV_LORA + ROPE_DIM:]
    shp = jax.ShapeDtypeStruct((_TAIL_CTX, POOL_WIDTH), F32)
    xpool, xconv = pl.pallas_call(_tail_in_kernel, out_shape=[shp, shp], compiler_params=_params(None))(
        h_ctx, lw['g_mix'], w_pool_glu)
    pooled, cact = _poolconv_prompt(xpool, xconv, lw, pos0=t - _TAIL_CTX, out_dtype=F32)
    lw32 = dict(lw, w_pool=w['w_pool'][l], w_conv_pw=w['w_conv_pw'][l], w_out=w['w_out'][l],
                w_mq=w['w_mq'][l] * MEM_SCALE, w_mk=w['w_mk'][l], w_mv=w['w_mv'][l], w_mo=w['w_mo'][l])
    keep = _TAIL_CTX - _TAIL_ROWS
    h1 = _mix_out(h_ctx[keep:], o_mla[t - _TAIL_ROWS:], pooled[keep:], cact[keep:], lw32)
    mk, mv = _mem_kv(mem, lw32)
    h2 = _cross_prompt(h1, mk, mv, lw32)
    return _route(h2, lw)[1]


def _rot_cols(w):
    half = ROPE_DIM // 2
    return jnp.concatenate([-w[..., half:], w[..., :half]], axis=-1)


def _rope_table(pos):
    half = ROPE_DIM // 2
    inv = ROPE_THETA ** (-jnp.arange(half, dtype=F32) / half)
    ang = pos.astype(F32)[:, None] * inv[None, :]
    cos, sin = jnp.cos(ang), jnp.sin(ang)
    return jnp.concatenate([cos, cos, sin, sin], axis=1)


def _layer_weights(l, w):
    o1, o2, o3, o4 = Q_LORA, Q_LORA + KV_LORA, Q_LORA + KV_LORA + ROPE_DIM, Q_LORA + KV_LORA + ROPE_DIM + POOL_WIDTH
    w_in = w['w_in'][l]
    w_kpe = w_in[:, o2:o3]
    w_in_r = jnp.concatenate([w_in[:, :o2], w_in[:, o3:], w_kpe, _rot_cols(w_kpe)], axis=1).astype(BF16)
    del o1, o4
    w_uq = w['w_uq'][l] * (MLA_SCALE * LOG2_E)
    w_pe = w_uq[..., NOPE_DIM:]
    w_q = jnp.concatenate([w_uq[..., :NOPE_DIM], w_pe, _rot_cols(w_pe)], axis=-1)
    w_q = w_q.reshape(Q_LORA, MLA_HEADS * Q_CHUNK).astype(BF16)
    w_uk = w['w_uk'][l]
    w_uv = w['w_uv'][l]
    w_route = jnp.zeros((D_MODEL, ROUTE_LANES), F32)
    w_route = w_route.at[:, :N_GROUPS].set(w['w_route_group'][l]).at[:, N_GROUPS:N_GROUPS + N_EXPERTS].set(w['w_route_expert'][l])
    b_route = jnp.zeros((1, ROUTE_LANES), F32)
    b_route = b_route.at[0, :N_GROUPS].set(w['b_route_group'][l]).at[0, N_GROUPS:N_GROUPS + N_EXPERTS].set(w['b_route_expert'][l])
    vec = lambda name: w[name][l].reshape(1, -1)
    return {
        'g_mix': vec('g_mix'), 'w_in': w_in_r, 'g_q_lat': vec('g_q_lat'), 'w_q': w_q, 'g_kv_lat': vec('g_kv_lat'),
        'w_uk': w_uk.reshape(KV_LORA, MLA_HEADS * NOPE_DIM).astype(BF16),
        'w_uv': w_uv.reshape(KV_LORA, MLA_HEADS * V_DIM).astype(BF16),
        'w_uk_t': jnp.transpose(w_uk, (1, 2, 0)).astype(BF16),
        'w_uv_h': jnp.transpose(w_uv, (1, 0, 2)).astype(BF16),
        'w_pool': w['w_pool'][l].astype(BF16), 'pool_scale': vec('pool_scale'),
        'w_conv_dw': w['w_conv_dw'][l], 'b_conv_dw': vec('b_conv_dw'),
        'g_conv_ln': vec('g_conv_ln'), 'b_conv_ln': vec('b_conv_ln'),
        'w_conv_pw': w['w_conv_pw'][l].astype(BF16), 'w_out': w['w_out'][l].astype(BF16),
        'g_mem_x': vec('g_mem_x'), 'g_mem_tok': vec('g_mem_tok'),
        'w_mq': (w['w_mq'][l] * MEM_SCALE).astype(BF16), 'w_mk': w['w_mk'][l].astype(BF16),
        'w_mv': w['w_mv'][l].astype(BF16), 'w_mo': w['w_mo'][l].astype(BF16),
        'g_ffn': vec('g_ffn'), 'w_route': w_route, 'b_route': b_route,
    }


def kernel(x_prompt, x_sample, mem_prompt, cache_ckv, cache_kpe, page_table, state_pool, state_conv, cache_mem_k, cache_mem_v, g_mix, w_in, g_q_lat, w_uq, g_kv_lat, w_uk, w_uv, w_pool, pool_scale, w_conv_dw, b_conv_dw, g_conv_ln, b_conv_ln, w_conv_pw, w_out, g_mem_x, g_mem_tok, w_mq, w_mk, w_mv, w_mo, g_ffn, w_route_group, b_route_group, w_route_expert, b_route_expert, w_gate, w_up, w_down, g_final):
    weights = dict(g_mix=g_mix, w_in=w_in, g_q_lat=g_q_lat, w_uq=w_uq, g_kv_lat=g_kv_lat, w_uk=w_uk, w_uv=w_uv,
                   w_pool=w_pool, pool_scale=pool_scale, w_conv_dw=w_conv_dw, b_conv_dw=b_conv_dw,
                   g_conv_ln=g_conv_ln, b_conv_ln=b_conv_ln, w_conv_pw=w_conv_pw, w_out=w_out, g_mem_x=g_mem_x,
                   g_mem_tok=g_mem_tok, w_mq=w_mq, w_mk=w_mk, w_mv=w_mv, w_mo=w_mo, g_ffn=g_ffn,
                   w_route_group=w_route_group, b_route_group=b_route_group, w_route_expert=w_route_expert,
                   b_route_expert=b_route_expert, w_gate=w_gate, w_up=w_up, w_down=w_down)
    n_prompt, t_prompt = x_prompt.shape[:2]
    n_dec, s_dec = x_sample.shape[:2]
    assert n_prompt == 1 and s_dec == 1
    tab_p = _rope_table(jnp.arange(t_prompt, dtype=jnp.int32))
    tab_s = _rope_table(jnp.full((n_dec,), PAST_LEN, jnp.int32))
    hp = x_prompt.reshape(t_prompt, D_MODEL)
    hs = x_sample.reshape(n_dec, D_MODEL)
    mem = mem_prompt.reshape(MEM_TOKENS, D_MODEL)
    g_fin = g_final.reshape(1, D_MODEL)
    cache_kpe_t = jnp.swapaxes(cache_kpe, 2, 3)
    mem_k_rows = cache_mem_k.reshape(DEPTH, n_dec, MEM_TOKENS * MEM_HEADS, MEM_HEAD_DIM)
    mem_v_rows = cache_mem_v.reshape(DEPTH, n_dec, MEM_TOKENS * MEM_HEADS, MEM_HEAD_DIM)
    outs = {k: [] for k in ('ckv_p', 'kpe_p', 'pool_p', 'conv_p', 'mk_p', 'mv_p', 'ckv_s', 'kpe_s', 'pool_s', 'conv_s')}
    for l in range(DEPTH):
        lw = _layer_weights(l, weights)
        last = l == DEPTH - 1

        q, k, v, ckv, kpe, xpool, xconv = _mixer_in(hp, tab_p, lw, prompt=True)
        o_mla = _flash_attention(q, k, v)
        pooled, cact = _poolconv_prompt(xpool, xconv, lw)
        rec_tail = None if last else _tail_route(l, weights, lw, hp, o_mla, mem)
        hp = _mix_out(hp, o_mla, pooled, cact, lw)
        outs['ckv_p'].append(ckv.reshape(1, t_prompt, KV_LORA))
        outs['kpe_p'].append(kpe.reshape(1, t_prompt, ROPE_DIM))
        outs['pool_p'].append(xpool[-POOL_STATE:].reshape(1, POOL_STATE, POOL_WIDTH))
        outs['conv_p'].append(xconv[-CONV_STATE:].reshape(1, CONV_STATE, CONV_WIDTH))
        mk, mv = _mem_kv(mem, lw)
        outs['mk_p'].append(mk.reshape(1, MEM_TOKENS, MEM_HEADS, MEM_HEAD_DIM))
        outs['mv_p'].append(mv.reshape(1, MEM_TOKENS, MEM_HEADS, MEM_HEAD_DIM))
        hp = _cross_prompt(hp, mk, mv, lw)

        qlat, qpe, ckv_s, kpe_s, xpool_s, xconv_s = _mixer_in(hs, tab_s, lw, prompt=False)
        o_lat = _paged_attention(l, page_table, jnp.transpose(qlat, (1, 0, 2)), jnp.transpose(qpe, (1, 0, 2)),
                                 ckv_s.reshape(n_dec, 1, KV_LORA), kpe_s.reshape(n_dec, 1, ROPE_DIM),
                                 cache_ckv, cache_kpe_t)
        o_mla_s = _value_up(jnp.transpose(o_lat, (1, 0, 2)), lw['w_uv_h'])
        pooled_s, cact_s = _poolconv_sample(jnp.transpose(state_pool[l], (1, 0, 2)),
                                            jnp.transpose(state_conv[l], (1, 0, 2)), xpool_s, xconv_s, lw)
        hs = _mix_out(hs, o_mla_s, pooled_s, cact_s, lw)
        outs['ckv_s'].append(ckv_s.reshape(n_dec, 1, KV_LORA))
        outs['kpe_s'].append(kpe_s.reshape(n_dec, 1, ROPE_DIM))
        outs['pool_s'].append(jnp.concatenate([state_pool[l][:, 1:], xpool_s[:, None, :]], axis=1))
        outs['conv_s'].append(jnp.concatenate([state_conv[l][:, 1:], xconv_s[:, None, :]], axis=1))
        hs = _cross_sample(l, hs, mem_k_rows, mem_v_rows, lw)

        hp, hs = _moe(l, hp, hs, lw, weights, g_fin, final_norm=last, rec_tail=rec_tail)

    stack = lambda name: jnp.stack(outs[name])
    return (hp.reshape(1, t_prompt, D_MODEL), hs.reshape(n_dec, 1, D_MODEL),
            stack('ckv_p'), stack('kpe_p'), stack('pool_p'), stack('conv_p'), stack('mk_p'), stack('mv_p'),
            stack('ckv_s'), stack('kpe_s'), stack('pool_s'), stack('conv_s'))
```

```python
import functools

import jax
import jax.numpy as jnp
from jax import lax
from jax.experimental import pallas as pl
from jax.experimental.pallas import tpu as pltpu

F32 = jnp.float32
BF16 = jnp.bfloat16

D_MODEL = 2048
DEPTH = 2
PAST_LEN = 16384
PAGE_SIZE = 128
NOPE_DIM = 128
ROPE_DIM = 64
V_DIM = 128
MLA_HEADS = 8
Q_LORA = 768
KV_LORA = 256
ROPE_THETA = 10000.0
MLA_SCALE = (NOPE_DIM + ROPE_DIM) ** -0.5
LOG2_E = 1.4426950408889634
POOL_WIDTH = 512
CONV_WIDTH = 512
POOL_WINDOWS = (2, 4, 8, 16)
POOL_GROUP_WIDTH = 128
POOL_STATE = 15
CONV_K = 31
CONV_STATE = 30
MEM_TOKENS = 256
MEM_HEADS = 4
MEM_HEAD_DIM = 128
MEM_WIDTH = 512
MEM_SCALE = MEM_HEAD_DIM ** -0.5
N_GROUPS = 4
EXPERTS_PER_GROUP = 8
N_EXPERTS = 32
D_EXPERT = 512
EPS = 1e-6
NEG_INF = -1e30

_O_CQ = Q_LORA
_O_CKV = _O_CQ + KV_LORA
_O_POOL = _O_CKV + POOL_WIDTH
_O_GLU = _O_POOL + 2 * CONV_WIDTH
IN_COLS_R = _O_GLU + 2 * ROPE_DIM
Q_CHUNK = NOPE_DIM + 2 * ROPE_DIM

VMEM_LIMIT_V7X = 56 * 1024 * 1024
ROUTE_LANES = 128
PAGES_PER_STEP = 64
PAGED_CHUNK_PAGES = 8
MIXER_ROWS = 256
FLASH_BLOCK = 512
FLASH_HEADS = 4
MOE_TILE = 256


def _params(sem):
    return pltpu.CompilerParams(dimension_semantics=sem, vmem_limit_bytes=VMEM_LIMIT_V7X)


def _const(shape):
    return pl.BlockSpec(shape, lambda *_: (0,) * len(shape), pipeline_mode=pl.Buffered(1))


def _rms(x, g):
    ms = jnp.mean(x * x, axis=-1, keepdims=True)
    return x * lax.rsqrt(ms + EPS) * g


def _precision(a):
    return lax.Precision.HIGHEST if a.dtype == F32 else None


def _dot(a, b):
    return jnp.dot(a, b, preferred_element_type=F32, precision=_precision(a))


def _dot_nt(a, b):
    return lax.dot_general(a, b, (((1,), (1,)), ((), ())), preferred_element_type=F32, precision=_precision(a))


def _mixer_common(h_ref, tab_ref, gmix_ref, win_ref, gq_ref, wq_ref, gkv_ref,
                  ckv_ref, kpe_ref, xpool_ref, xconv_ref):
    xn = _rms(h_ref[...], gmix_ref[...]).astype(BF16)
    u = _dot(xn, win_ref[...])
    cq = _rms(u[:, :_O_CQ], gq_ref[...]).astype(BF16)
    ckv = _rms(u[:, _O_CQ:_O_CKV], gkv_ref[...])
    ckv_ref[...] = ckv
    xpool_ref[...] = u[:, _O_CKV:_O_POOL]
    xconv_ref[...] = u[:, _O_POOL:_O_POOL + CONV_WIDTH] * jax.nn.sigmoid(u[:, _O_POOL + CONV_WIDTH:_O_GLU])
    tab = tab_ref[...]
    kp = u[:, _O_GLU:] * tab
    kp2 = kp + pltpu.roll(kp, ROPE_DIM, 1)
    kpe_ref[...] = kp2[:, :ROPE_DIM]
    qraw = _dot(cq, wq_ref[...])
    return ckv, kp2, qraw, tab


def _mixer_prompt_kernel(h_ref, tab_ref, gmix_ref, win_ref, gq_ref, wq_ref, gkv_ref, wuk_ref, wuv_ref,
                         q_ref, k_ref, v_ref, ckv_ref, kpe_ref, xpool_ref, xconv_ref):
    ckv, kp2, qraw, tab = _mixer_common(h_ref, tab_ref, gmix_ref, win_ref, gq_ref, wq_ref, gkv_ref,
                                        ckv_ref, kpe_ref, xpool_ref, xconv_ref)
    ckv_b = ckv.astype(BF16)
    kn = _dot(ckv_b, wuk_ref[...])
    vv = _dot(ckv_b, wuv_ref[...])
    kp2b = kp2.astype(BF16)
    for hh in range(MLA_HEADS):
        c0 = Q_CHUNK * hh
        q_ref[hh, :, :NOPE_DIM] = qraw[:, c0:c0 + NOPE_DIM].astype(BF16)
        q_ref[hh, :, NOPE_DIM:] = (qraw[:, c0 + NOPE_DIM:c0 + Q_CHUNK] * tab).astype(BF16)
        k_ref[hh, :, :NOPE_DIM] = kn[:, NOPE_DIM * hh:NOPE_DIM * (hh + 1)].astype(BF16)
        k_ref[hh, :, NOPE_DIM:] = kp2b
        v_ref[hh, 0] = vv[:, V_DIM * hh:V_DIM * (hh + 1)].T.astype(BF16)


def _mixer_sample_kernel(h_ref, tab_ref, gmix_ref, win_ref, gq_ref, wq_ref, gkv_ref, wukt_ref,
                         qlat_ref, qpe_ref, ckv_ref, kpe_ref, xpool_ref, xconv_ref):
    _, _, qraw, tab = _mixer_common(h_ref, tab_ref, gmix_ref, win_ref, gq_ref, wq_ref, gkv_ref,
                                    ckv_ref, kpe_ref, xpool_ref, xconv_ref)
    for hh in range(MLA_HEADS):
        c0 = Q_CHUNK * hh
        qn = qraw[:, c0:c0 + NOPE_DIM].astype(BF16)
        qlat_ref[hh] = _dot(qn, wukt_ref[hh]).astype(BF16)
        qp = qraw[:, c0 + NOPE_DIM:c0 + Q_CHUNK] * tab
        qp2 = qp + pltpu.roll(qp, ROPE_DIM, 1)
        qpe_ref[hh] = qp2[:, :ROPE_DIM].astype(BF16)


def _mixer_in(h, tab, lw, *, prompt):
    t = h.shape[0]
    tm = MIXER_ROWS if prompt else t
    row = lambda w: pl.BlockSpec((tm, w), lambda i: (i, 0))
    head = lambda w: pl.BlockSpec((MLA_HEADS, tm, w), lambda i: (0, i, 0))
    in_specs = [row(D_MODEL), row(2 * ROPE_DIM), _const((1, D_MODEL)), _const((D_MODEL, IN_COLS_R)),
                _const((1, Q_LORA)), _const((Q_LORA, MLA_HEADS * Q_CHUNK)), _const((1, KV_LORA))]
    args = [h, tab, lw['g_mix'], lw['w_in'], lw['g_q_lat'], lw['w_q'], lw['g_kv_lat']]
    tail_specs = [row(KV_LORA), row(ROPE_DIM), row(POOL_WIDTH), row(CONV_WIDTH)]
    tail_shapes = [jax.ShapeDtypeStruct((t, KV_LORA), F32), jax.ShapeDtypeStruct((t, ROPE_DIM), F32),
                   jax.ShapeDtypeStruct((t, POOL_WIDTH), F32), jax.ShapeDtypeStruct((t, CONV_WIDTH), F32)]
    if prompt:
        kern = _mixer_prompt_kernel
        in_specs += [_const((KV_LORA, MLA_HEADS * NOPE_DIM)), _const((KV_LORA, MLA_HEADS * V_DIM))]
        args += [lw['w_uk'], lw['w_uv']]
        vt_spec = pl.BlockSpec((MLA_HEADS, 1, V_DIM, tm), lambda i: (0, i, 0, 0))
        out_specs = [head(2 * NOPE_DIM), head(2 * NOPE_DIM), vt_spec] + tail_specs
        out_shape = [jax.ShapeDtypeStruct((MLA_HEADS, t, 2 * NOPE_DIM), BF16),
                     jax.ShapeDtypeStruct((MLA_HEADS, t, 2 * NOPE_DIM), BF16),
                     jax.ShapeDtypeStruct((MLA_HEADS, t // tm, V_DIM, tm), BF16)] + tail_shapes
    else:
        kern = _mixer_sample_kernel
        in_specs += [_const((MLA_HEADS, NOPE_DIM, KV_LORA))]
        args += [lw['w_uk_t']]
        out_specs = [head(KV_LORA), head(ROPE_DIM)] + tail_specs
        out_shape = [jax.ShapeDtypeStruct((MLA_HEADS, t, KV_LORA), BF16),
                     jax.ShapeDtypeStruct((MLA_HEADS, t, ROPE_DIM), BF16)] + tail_shapes
    return pl.pallas_call(kern, grid=(t // tm,), in_specs=in_specs, out_specs=out_specs,
                          out_shape=out_shape, compiler_params=_params(("parallel",)))(*args)


def _flash_kernel(q_ref, k_ref, vt_ref, o_ref, m_scr, l_scr, acc_scr):
    blk = FLASH_BLOCK
    sub = blk // MIXER_ROWS
    qi = pl.program_id(1)
    m_scr[...] = jnp.full_like(m_scr, -jnp.inf)
    l_scr[...] = jnp.zeros_like(l_scr)
    acc_scr[...] = jnp.zeros_like(acc_scr)

    def step(kv, diagonal):
        start = pl.multiple_of(kv * blk, blk)
        for hh in range(FLASH_HEADS):
            k = k_ref[hh, pl.ds(start, blk), :]
            st = _dot_nt(k, q_ref[hh])
            if diagonal:
                key = lax.broadcasted_iota(jnp.int32, (blk, blk), 0)
                qry = lax.broadcasted_iota(jnp.int32, (blk, blk), 1)
                st = jnp.where(key <= qry, st, NEG_INF)
            m_prev = m_scr[hh]
            m_new = jnp.maximum(m_prev, jnp.max(st, axis=0, keepdims=True))
            p = jnp.exp2(st - m_new)
            alpha = jnp.exp2(m_prev - m_new)
            l_scr[hh] = alpha * l_scr[hh] + jnp.sum(p, axis=0, keepdims=True)
            pb = p.astype(BF16)
            pv = _dot(vt_ref[hh, kv * sub], pb[:MIXER_ROWS])
            for c in range(1, sub):
                pv = pv + _dot(vt_ref[hh, kv * sub + c], pb[c * MIXER_ROWS:(c + 1) * MIXER_ROWS])
            acc_scr[hh] = alpha * acc_scr[hh] + pv
            m_scr[hh] = m_new

    def body(kv, carry):
        step(kv, False)
        return carry

    lax.fori_loop(0, qi, body, 0)
    step(qi, True)
    for hh in range(FLASH_HEADS):
        o_ref[:, V_DIM * hh:V_DIM * (hh + 1)] = (acc_scr[hh] / l_scr[hh]).T.astype(o_ref.dtype)


def _flash_attention(q, k, vt):
    nh, t, dk = q.shape
    blk, g = FLASH_BLOCK, FLASH_HEADS
    return pl.pallas_call(
        _flash_kernel,
        grid=(nh // g, t // blk),
        in_specs=[pl.BlockSpec((g, blk, dk), lambda h, i: (h, i, 0)),
                  pl.BlockSpec((g, t, dk), lambda h, i: (h, 0, 0), pipeline_mode=pl.Buffered(1)),
                  pl.BlockSpec((g, t // MIXER_ROWS, V_DIM, MIXER_ROWS), lambda h, i: (h, 0, 0, 0),
                               pipeline_mode=pl.Buffered(1))],
        out_specs=pl.BlockSpec((blk, g * V_DIM), lambda h, i: (i, h)),
        out_shape=jax.ShapeDtypeStruct((t, nh * V_DIM), F32),
        scratch_shapes=[pltpu.VMEM((g, 1, blk), F32), pltpu.VMEM((g, 1, blk), F32), pltpu.VMEM((g, V_DIM, blk), F32)],
        compiler_params=_params(("parallel", "parallel")))(q, k, vt)


def _page_copies(pt_ref, ckv_hbm, kpet_hbm, ckv_stage, kpet_stage, sems, layer, seq, step, slot):
    copies = []
    for i in range(PAGES_PER_STEP):
        page = pt_ref[seq, step * PAGES_PER_STEP + i]
        tok = pl.ds(i * PAGE_SIZE, PAGE_SIZE)
        copies.append(pltpu.make_async_copy(ckv_hbm.at[layer, page], ckv_stage.at[slot, tok, :], sems.at[0, slot]))
        copies.append(pltpu.make_async_copy(kpet_hbm.at[layer, page], kpet_stage.at[slot, :, tok], sems.at[1, slot]))
    return copies


def _paged_kernel(pt_ref, qlat_ref, qpe_ref, ckvn_ref, kpen_ref, ckv_hbm, kpet_hbm, o_ref,
                  m_scr, l_scr, acc_scr, ckv_stage, kpet_stage, sems, *, layer, n_seqs, n_steps):
    b = pl.program_id(0)
    j = pl.program_id(1)
    n = b * n_steps + j
    slot = lax.rem(n, 2)
    copies = functools.partial(_page_copies, pt_ref, ckv_hbm, kpet_hbm, ckv_stage, kpet_stage, sems, layer)

    @pl.when(n == 0)
    def _():
        for c in copies(0, 0, 0):
            c.start()

    @pl.when(n + 1 < n_seqs * n_steps)
    def _():
        wrap = j == n_steps - 1
        for c in copies(jnp.where(wrap, b + 1, b), jnp.where(wrap, 0, j + 1), 1 - slot):
            c.start()

    ql = qlat_ref[0]
    qp = qpe_ref[0]

    @pl.when(j == 0)
    def _():
        cn = ckvn_ref[0]
        kn = kpen_ref[0]
        s_self = (jnp.sum(ql.astype(F32) * cn, axis=1, keepdims=True)
                  + jnp.sum(qp.astype(F32) * kn, axis=1, keepdims=True))
        m_scr[...] = s_self
        l_scr[...] = jnp.ones_like(l_scr)
        acc_scr[...] = jnp.broadcast_to(cn, acc_scr.shape)

    for c in copies(b, j, slot):
        c.wait()

    chunk = PAGED_CHUNK_PAGES * PAGE_SIZE
    kcs, ss = [], []
    for c in range(PAGES_PER_STEP // PAGED_CHUNK_PAGES):
        kc = ckv_stage[slot, c * chunk:(c + 1) * chunk, :].astype(BF16)
        kp = kpet_stage[slot, :, c * chunk:(c + 1) * chunk].astype(BF16)
        kcs.append(kc)
        ss.append(_dot_nt(ql, kc) + _dot(qp, kp))
    s = jnp.concatenate(ss, axis=1)
    m_prev = m_scr[...]
    m_new = jnp.maximum(m_prev, jnp.max(s, axis=1, keepdims=True))
    p = jnp.exp2(s - m_new)
    alpha = jnp.exp2(m_prev - m_new)
    l_scr[...] = alpha * l_scr[...] + jnp.sum(p, axis=1, keepdims=True)
    pb = p.astype(BF16)
    pv = _dot(pb[:, :chunk], kcs[0])
    for c in range(1, len(kcs)):
        pv = pv + _dot(pb[:, c * chunk:(c + 1) * chunk], kcs[c])
    acc_scr[...] = alpha * acc_scr[...] + pv
    m_scr[...] = m_new

    @pl.when(j == n_steps - 1)
    def _():
        o_ref[0] = acc_scr[...] / l_scr[...]


def _paged_attention(layer, page_table, qlat, qpe, ckv_new, kpe_new, cache_ckv, cache_kpe_t):
    s, n_pages = page_table.shape
    n_steps = n_pages // PAGES_PER_STEP
    step_tokens = PAGES_PER_STEP * PAGE_SIZE
    per_seq = lambda r, w: pl.BlockSpec((1, r, w), lambda b, j, pt: (b, 0, 0))
    hbm = pl.BlockSpec(memory_space=pl.ANY)
    grid_spec = pltpu.PrefetchScalarGridSpec(
        num_scalar_prefetch=1, grid=(s, n_steps),
        in_specs=[per_seq(MLA_HEADS, KV_LORA), per_seq(MLA_HEADS, ROPE_DIM), per_seq(1, KV_LORA), per_seq(1, ROPE_DIM),
                  hbm, hbm],
        out_specs=per_seq(MLA_HEADS, KV_LORA),
        scratch_shapes=[pltpu.VMEM((MLA_HEADS, 1), F32), pltpu.VMEM((MLA_HEADS, 1), F32),
                        pltpu.VMEM((MLA_HEADS, KV_LORA), F32),
                        pltpu.VMEM((2, step_tokens, KV_LORA), F32), pltpu.VMEM((2, ROPE_DIM, step_tokens), F32),
                        pltpu.SemaphoreType.DMA((2, 2))])
    return pl.pallas_call(
        functools.partial(_paged_kernel, layer=layer, n_seqs=s, n_steps=n_steps), grid_spec=grid_spec,
        out_shape=jax.ShapeDtypeStruct((s, MLA_HEADS, KV_LORA), F32),
        compiler_params=_params(("arbitrary", "arbitrary")))(
            page_table, qlat, qpe, ckv_new, kpe_new, cache_ckv, cache_kpe_t)


def _value_up_kernel(olat_ref, wuv_ref, o_ref):
    for hh in range(MLA_HEADS):
        o_ref[:, V_DIM * hh:V_DIM * (hh + 1)] = _dot(olat_ref[hh].astype(BF16), wuv_ref[hh]).astype(o_ref.dtype)


def _value_up(olat_h, w_uv_h):
    s = olat_h.shape[1]
    return pl.pallas_call(_value_up_kernel, out_shape=jax.ShapeDtypeStruct((s, MLA_HEADS * V_DIM), BF16))(olat_h, w_uv_h)


def _ln_silu(y, g, b):
    mu = jnp.mean(y, axis=-1, keepdims=True)
    d = y - mu
    var = jnp.mean(d * d, axis=-1, keepdims=True)
    z = d * lax.rsqrt(var + EPS) * g + b
    return z * jax.nn.sigmoid(z)


_POOL_PAD = 16
_CONV_PAD = 32


def _poolconv_prompt_kernel(xp_ref, xc_ref, wdw_ref, bdw_ref, gln_ref, bln_ref, pooled_ref, cact_ref,
                            extp, extc, *, tm, pos0):
    i = pl.program_id(0)

    @pl.when(i == 0)
    def _():
        extp[0:_POOL_PAD, :] = jnp.zeros((_POOL_PAD, POOL_WIDTH), F32)
        extc[0:_CONV_PAD, :] = jnp.zeros((_CONV_PAD, CONV_WIDTH), F32)

    extp[_POOL_PAD:, :] = xp_ref[...]
    extc[_CONV_PAD:, :] = xc_ref[...]
    pos = pos0 + i * tm + lax.broadcasted_iota(jnp.int32, (tm, 1), 0)
    for gi, w in enumerate(POOL_WINDOWS):
        c0 = gi * POOL_GROUP_WIDTH
        x_new = extp[_POOL_PAD:, c0:c0 + POOL_GROUP_WIDTH]
        acc = x_new
        for k in range(1, w):
            acc = acc + extp[_POOL_PAD - k:_POOL_PAD - k + tm, c0:c0 + POOL_GROUP_WIDTH]
        cnt = jnp.minimum(pos + 1, w).astype(F32)
        pooled_ref[:, c0:c0 + POOL_GROUP_WIDTH] = (acc / cnt - x_new).astype(pooled_ref.dtype)
    base = _CONV_PAD - CONV_STATE
    acc = extc[base:base + tm, :] * wdw_ref[0:1, :]
    for k in range(1, CONV_K):
        acc = acc + extc[base + k:base + k + tm, :] * wdw_ref[k:k + 1, :]
    cact_ref[...] = _ln_silu(acc + bdw_ref[...], gln_ref[...], bln_ref[...]).astype(cact_ref.dtype)
    extp[0:_POOL_PAD, :] = extp[tm:tm + _POOL_PAD, :]
    extc[0:_CONV_PAD, :] = extc[tm:tm + _CONV_PAD, :]


def _poolconv_prompt(xpool, xconv, lw, *, pos0=0, out_dtype=BF16):
    t = xpool.shape[0]
    tm = min(t, 512)
    row = pl.BlockSpec((tm, POOL_WIDTH), lambda i: (i, 0))
    vec = _const((1, CONV_WIDTH))
    return pl.pallas_call(
        functools.partial(_poolconv_prompt_kernel, tm=tm, pos0=pos0), grid=(t // tm,),
        in_specs=[row, row, _const((CONV_K, CONV_WIDTH)), vec, vec, vec],
        out_specs=[row, row],
        out_shape=[jax.ShapeDtypeStruct((t, POOL_WIDTH), out_dtype), jax.ShapeDtypeStruct((t, CONV_WIDTH), out_dtype)],
        scratch_shapes=[pltpu.VMEM((tm + _POOL_PAD, POOL_WIDTH), F32), pltpu.VMEM((tm + _CONV_PAD, CONV_WIDTH), F32)],
        compiler_params=_params(("arbitrary",)))(
            xpool, xconv, lw['w_conv_dw'], lw['b_conv_dw'], lw['g_conv_ln'], lw['b_conv_ln'])


def _poolconv_sample_kernel(sp_ref, sc_ref, xp_ref, xc_ref, wdw_ref, bdw_ref, gln_ref, bln_ref,
                            pooled_ref, cact_ref):
    for gi, w in enumerate(POOL_WINDOWS):
        c0 = gi * POOL_GROUP_WIDTH
        x_new = xp_ref[:, c0:c0 + POOL_GROUP_WIDTH]
        acc = x_new
        for k in range(1, w):
            acc = acc + sp_ref[POOL_STATE - k, :, c0:c0 + POOL_GROUP_WIDTH]
        cnt = float(min(PAST_LEN + 1, w))
        pooled_ref[:, c0:c0 + POOL_GROUP_WIDTH] = (acc / cnt - x_new).astype(pooled_ref.dtype)
    acc = xc_ref[...] * wdw_ref[CONV_K - 1:CONV_K, :]
    for k in range(CONV_STATE):
        acc = acc + sc_ref[k] * wdw_ref[k:k + 1, :]
    cact_ref[...] = _ln_silu(acc + bdw_ref[...], gln_ref[...], bln_ref[...]).astype(cact_ref.dtype)


def _poolconv_sample(state_pool_t, state_conv_t, xpool, xconv, lw):
    s = xpool.shape[0]
    return pl.pallas_call(
        _poolconv_sample_kernel,
        out_shape=[jax.ShapeDtypeStruct((s, POOL_WIDTH), BF16), jax.ShapeDtypeStruct((s, CONV_WIDTH), BF16)])(
            state_pool_t, state_conv_t, xpool, xconv,
            lw['w_conv_dw'], lw['b_conv_dw'], lw['g_conv_ln'], lw['b_conv_ln'])


def _mix_out_kernel(h_ref, omla_ref, pooled_ref, cact_ref, wpool_ref, pscale_ref, wpw_ref, wout_ref, o_ref):
    cdt = wout_ref.dtype
    acc = h_ref[...] + _dot(omla_ref[...].astype(cdt), wout_ref[0:1024, :])
    pooled = pooled_ref[...]
    for gi in range(len(POOL_WINDOWS)):
        c0 = gi * POOL_GROUP_WIDTH
        og = _dot(pooled[:, c0:c0 + POOL_GROUP_WIDTH], wpool_ref[gi]) * pscale_ref[:, c0:c0 + POOL_GROUP_WIDTH]
        acc = acc + _dot(og.astype(cdt), wout_ref[1024 + c0:1024 + c0 + POOL_GROUP_WIDTH, :])
    oc = _dot(cact_ref[...], wpw_ref[...]).astype(cdt)
    o_ref[...] = acc + _dot(oc, wout_ref[1536:2048, :])


def _mix_out(h, omla, pooled, cact, lw):
    t = h.shape[0]
    tm = min(t, 512)
    row = lambda w: pl.BlockSpec((tm, w), lambda i: (i, 0))
    return pl.pallas_call(
        _mix_out_kernel, grid=(t // tm,),
        in_specs=[row(D_MODEL), row(1024), row(POOL_WIDTH), row(CONV_WIDTH),
                  _const((len(POOL_WINDOWS), POOL_GROUP_WIDTH, POOL_GROUP_WIDTH)), _const((1, POOL_WIDTH)),
                  _const((CONV_WIDTH, CONV_WIDTH)), _const((D_MODEL, D_MODEL))],
        out_specs=row(D_MODEL), out_shape=jax.ShapeDtypeStruct((t, D_MODEL), F32),
        compiler_params=_params(("parallel",)))(
            h, omla, pooled, cact, lw['w_pool'], lw['pool_scale'], lw['w_conv_pw'], lw['w_out'])


def _mem_kv_kernel(mem_ref, g_ref, wk_ref, wv_ref, k_ref, v_ref):
    m = _rms(mem_ref[...], g_ref[...]).astype(wk_ref.dtype)
    k_ref[...] = _dot(m, wk_ref[...])
    v_ref[...] = _dot(m, wv_ref[...])


def _mem_kv(mem, lw):
    n = mem.shape[0]
    shp = jax.ShapeDtypeStruct((n, MEM_WIDTH), F32)
    return pl.pallas_call(_mem_kv_kernel, out_shape=[shp, shp],
                          compiler_params=_params(None))(mem, lw['g_mem_tok'], lw['w_mk'], lw['w_mv'])


def _cross_prompt_kernel(h_ref, g_ref, wq_ref, mk_ref, mv_ref, wo_ref, o_ref):
    cdt = wq_ref.dtype
    h = h_ref[...]
    q = _dot(_rms(h, g_ref[...]).astype(cdt), wq_ref[...]).astype(cdt)
    mk = mk_ref[...].astype(cdt)
    mv = mv_ref[...].astype(cdt)
    acc = h
    for hh in range(MEM_HEADS):
        c = slice(MEM_HEAD_DIM * hh, MEM_HEAD_DIM * (hh + 1))
        s = _dot_nt(q[:, c], mk[:, c])
        p = jnp.exp(s - jnp.max(s, axis=1, keepdims=True))
        p = p / jnp.sum(p, axis=1, keepdims=True)
        oh = _dot(p.astype(cdt), mv[:, c]).astype(cdt)
        acc = acc + _dot(oh, wo_ref[c, :])
    o_ref[...] = acc


def _cross_prompt(h, mk, mv, lw):
    t = h.shape[0]
    tm = min(t, 512)
    row = pl.BlockSpec((tm, D_MODEL), lambda i: (i, 0))
    return pl.pallas_call(
        _cross_prompt_kernel, grid=(t // tm,),
        in_specs=[row, _const((1, D_MODEL)), _const((D_MODEL, MEM_WIDTH)), _const((MEM_TOKENS, MEM_WIDTH)),
                  _const((MEM_TOKENS, MEM_WIDTH)), _const((MEM_WIDTH, D_MODEL))],
        out_specs=row, out_shape=jax.ShapeDtypeStruct((t, D_MODEL), F32),
        compiler_params=_params(("parallel",)))(h, lw['g_mem_x'], lw['w_mq'], mk, mv, lw['w_mo'])


_CROSS_SEQS = 8


def _cross_sample_kernel(h_ref, g_ref, wq_ref, k_ref, v_ref, wo_ref, o_ref):
    h = h_ref[...]
    q = _dot(_rms(h, g_ref[...]).astype(BF16), wq_ref[...])
    rows = 8
    n_rows = MEM_TOKENS * MEM_HEADS
    col_head = jnp.bitwise_and(lax.broadcasted_iota(jnp.int32, (rows, n_rows), 1), MEM_HEADS - 1)
    own = col_head == lax.broadcasted_iota(jnp.int32, (rows, n_rows), 0)
    pad = jnp.zeros((rows - MEM_HEADS, MEM_HEAD_DIM), F32)
    outs = []
    for b in range(_CROSS_SEQS):
        qb = jnp.concatenate([q[b:b + 1, MEM_HEAD_DIM * hh:MEM_HEAD_DIM * (hh + 1)] for hh in range(MEM_HEADS)]
                             + [pad], axis=0).astype(BF16)
        kb = k_ref[b].astype(BF16)
        vb = v_ref[b].astype(BF16)
        s = jnp.where(own, _dot_nt(qb, kb), NEG_INF)
        p = jnp.exp(s - jnp.max(s, axis=1, keepdims=True))
        p = p / jnp.sum(p, axis=1, keepdims=True)
        ob = _dot(p.astype(BF16), vb)
        outs.append(jnp.concatenate([ob[hh:hh + 1, :] for hh in range(MEM_HEADS)], axis=1))
    o = jnp.concatenate(outs, axis=0).astype(BF16)
    o_ref[...] = h + _dot(o, wo_ref[...])


def _cross_sample(layer, h, mem_k, mem_v, lw):
    s = h.shape[0]
    row = pl.BlockSpec((_CROSS_SEQS, D_MODEL), lambda i: (i, 0))
    kv = pl.BlockSpec((None, _CROSS_SEQS, MEM_TOKENS * MEM_HEADS, MEM_HEAD_DIM), lambda i: (layer, i, 0, 0))
    return pl.pallas_call(
        _cross_sample_kernel, grid=(s // _CROSS_SEQS,),
        in_specs=[row, _const((1, D_MODEL)), _const((D_MODEL, MEM_WIDTH)), kv, kv, _const((MEM_WIDTH, D_MODEL))],
        out_specs=row, out_shape=jax.ShapeDtypeStruct((s, D_MODEL), F32),
        compiler_params=_params(("parallel",)))(h, lw['g_mem_x'], lw['w_mq'], mem_k, mem_v, lw['w_mo'])


def _route_kernel(h_ref, g_ref, wr_ref, br_ref, xn_ref, rec_ref):
    xn = _rms(h_ref[...], g_ref[...])
    xn_ref[...] = xn
    logits = jnp.dot(xn, wr_ref[...], preferred_element_type=F32, precision=lax.Precision.HIGHEST) + br_ref[...]
    col = lax.broadcasted_iota(jnp.int32, logits.shape, 1)
    big = jnp.int32(ROUTE_LANES)
    is_group = col < N_GROUPS
    lg = jnp.where(is_group, logits, NEG_INF)
    g_max = jnp.max(lg, axis=1, keepdims=True)
    g_sel = jnp.min(jnp.where(lg == g_max, col, big), axis=1, keepdims=True)
    p_sel = 1.0 / jnp.sum(jnp.where(is_group, jnp.exp(lg - g_max), 0.0), axis=1, keepdims=True)
    e_col = col - N_GROUPS
    e_group = jnp.right_shift(e_col, EXPERTS_PER_GROUP.bit_length() - 1)
    in_group = (e_col >= 0) & (e_col < N_EXPERTS) & (e_group == g_sel)
    le = jnp.where(in_group, logits, NEG_INF)
    v1 = jnp.max(le, axis=1, keepdims=True)
    i1 = jnp.min(jnp.where(le == v1, col, big), axis=1, keepdims=True)
    le2 = jnp.where(col == i1, NEG_INF, le)
    v2 = jnp.max(le2, axis=1, keepdims=True)
    i2 = jnp.min(jnp.where(le2 == v2, col, big), axis=1, keepdims=True)
    e2 = jnp.exp(v2 - v1)
    w1 = p_sel / (1.0 + e2)
    w2 = p_sel * e2 / (1.0 + e2)
    rec = jnp.where(col == 0, (i1 - N_GROUPS).astype(F32),
                    jnp.where(col == 1, (i2 - N_GROUPS).astype(F32),
                              jnp.where(col == 2, w1, jnp.where(col == 3, w2, 0.0))))
    rec_ref[...] = rec


def _route(h, lw):
    t = h.shape[0]
    tm = min(t, 256)
    row = lambda w: pl.BlockSpec((tm, w), lambda i: (i, 0))
    return pl.pallas_call(
        _route_kernel, grid=(t // tm,),
        in_specs=[row(D_MODEL), _const((1, D_MODEL)), _const((D_MODEL, ROUTE_LANES)), _const((1, ROUTE_LANES))],
        out_specs=[row(D_MODEL), row(ROUTE_LANES)],
        out_shape=[jax.ShapeDtypeStruct((t, D_MODEL), F32), jax.ShapeDtypeStruct((t, ROUTE_LANES), F32)],
        compiler_params=_params(("parallel",)))(h, lw['g_ffn'], lw['w_route'], lw['b_route'])


def _tile_row_copy(tok_ref, x_hbm, xbuf, sem, tile, slot, r):
    tok = tok_ref[tile * MOE_TILE + r]
    return pltpu.make_async_copy(x_hbm.at[pl.ds(tok, 1)], xbuf.at[slot, pl.ds(r, 1)], sem.at[slot])


def _experts_kernel(te_ref, nt_ref, tok_ref, x_hbm, wg_ref, wu_ref, wd_ref, o_ref, wg_b, wu_b, wd_b, xbuf, sem):
    i = pl.program_id(0)
    n_used = nt_ref[0]
    used = i < n_used
    slot = lax.rem(i, 2)
    new_expert = (i == 0) | (te_ref[i] != te_ref[jnp.maximum(i - 1, 0)])
    row_copy = functools.partial(_tile_row_copy, tok_ref, x_hbm, xbuf, sem)

    def start_tile(tile, sl):
        def body(r, carry):
            row_copy(tile, sl, r).start()
            return carry
        lax.fori_loop(0, MOE_TILE, body, 0, unroll=8)

    @pl.when((i == 0) & used)
    def _():
        start_tile(0, 0)

    @pl.when(i + 1 < n_used)
    def _():
        start_tile(i + 1, 1 - slot)

    @pl.when(used & new_expert)
    def _():
        wg_b[...] = wg_ref[...].astype(BF16)
        wu_b[...] = wu_ref[...].astype(BF16)
        wd_b[...] = wd_ref[...].astype(BF16)

    @pl.when(used)
    def _():
        def body(r, carry):
            row_copy(i, slot, r).wait()
            return carry
        lax.fori_loop(0, MOE_TILE, body, 0, unroll=8)
        x = xbuf[slot].astype(BF16)
        hg = _dot(x, wg_b[...])
        hu = _dot(x, wu_b[...])
        act = (hg * jax.nn.sigmoid(hg) * hu).astype(BF16)
        o_ref[...] = _dot(act, wd_b[...])

    @pl.when(jnp.logical_not(used))
    def _():
        o_ref[...] = jnp.zeros_like(o_ref)


def _experts(layer, tile_expert, n_tiles_used, slot_token, x_rows, w_gate, w_up, w_down):
    r = slot_token.shape[0]
    n_tiles = r // MOE_TILE
    w_spec = lambda a, b: pl.BlockSpec((None, None, a, b), lambda i, te, nt, tok: (layer, te[i], 0, 0))
    grid_spec = pltpu.PrefetchScalarGridSpec(
        num_scalar_prefetch=3, grid=(n_tiles,),
        in_specs=[pl.BlockSpec(memory_space=pl.ANY),
                  w_spec(D_MODEL, D_EXPERT), w_spec(D_MODEL, D_EXPERT), w_spec(D_EXPERT, D_MODEL)],
        out_specs=pl.BlockSpec((MOE_TILE, D_MODEL), lambda i, te, nt, tok: (i, 0)),
        scratch_shapes=[pltpu.VMEM((D_MODEL, D_EXPERT), BF16), pltpu.VMEM((D_MODEL, D_EXPERT), BF16),
                        pltpu.VMEM((D_EXPERT, D_MODEL), BF16),
                        pltpu.VMEM((2, MOE_TILE, D_MODEL), F32), pltpu.SemaphoreType.DMA((2,))])
    return pl.pallas_call(
        _experts_kernel, grid_spec=grid_spec, out_shape=jax.ShapeDtypeStruct((r, D_MODEL), F32),
        compiler_params=_params(("arbitrary",)))(
            tile_expert, n_tiles_used, slot_token, x_rows, w_gate, w_up, w_down)


def _combine_kernel(h_ref, a_ref, b_ref, rec_ref, gf_ref, o_ref, *, final_norm):
    rec = rec_ref[...]
    y = h_ref[...] + rec[:, 2:3] * a_ref[...] + rec[:, 3:4] * b_ref[...]
    o_ref[...] = _rms(y, gf_ref[...]) if final_norm else y


def _combine(h, ya, yb, rec, g_final, *, final_norm):
    t = h.shape[0]
    tm = min(t, 512)
    row = lambda w: pl.BlockSpec((tm, w), lambda i: (i, 0))
    return pl.pallas_call(
        functools.partial(_combine_kernel, final_norm=final_norm), grid=(t // tm,),
        in_specs=[row(D_MODEL), row(D_MODEL), row(D_MODEL), row(ROUTE_LANES), _const((1, D_MODEL))],
        out_specs=row(D_MODEL), out_shape=jax.ShapeDtypeStruct((t, D_MODEL), F32),
        compiler_params=_params(("parallel",)))(h, ya, yb, rec, g_final)


def _dispatch_plan(expert_ids):
    n_pairs = expert_ids.shape[0]
    n_rows = -(-(n_pairs + N_EXPERTS * (MOE_TILE - 1)) // MOE_TILE) * MOE_TILE
    one_hot = (expert_ids[:, None] == jnp.arange(N_EXPERTS, dtype=jnp.int32)[None, :]).astype(jnp.int32)
    running = jnp.cumsum(one_hot, axis=0)
    rank = jnp.sum(running * one_hot, axis=1) - 1
    counts = running[-1]
    padded = -(-counts // MOE_TILE) * MOE_TILE
    ends = jnp.cumsum(padded)
    starts = ends - padded
    pair_slot = jnp.sum(one_hot * starts[None, :], axis=1) + rank
    slot_pair = jnp.zeros((n_rows,), jnp.int32).at[pair_slot].set(jnp.arange(n_pairs, dtype=jnp.int32),
                                                                   unique_indices=True)
    tile_start = jnp.arange(n_rows // MOE_TILE, dtype=jnp.int32) * MOE_TILE
    tile_expert = jnp.sum((tile_start[:, None] >= ends[None, :]).astype(jnp.int32), axis=1)
    tile_expert = jnp.minimum(tile_expert, N_EXPERTS - 1)
    n_tiles_used = (ends[-1] // MOE_TILE).astype(jnp.int32).reshape(1)
    return pair_slot, slot_pair, tile_expert, n_tiles_used


def _moe(layer, hp, hs, lw, w, g_final, *, final_norm, rec_tail=None):
    xn_p, rec_p = _route(hp, lw)
    xn_s, rec_s = _route(hs, lw)
    tp = hp.shape[0]
    if rec_tail is not None:
        rec_p = jnp.concatenate([rec_p[:tp - rec_tail.shape[0]], rec_tail], axis=0)
    rec = jnp.concatenate([rec_p, rec_s], axis=0)
    expert_ids = rec[:, :2].astype(jnp.int32).reshape(-1)
    pair_slot, slot_pair, tile_expert, n_tiles_used = _dispatch_plan(expert_ids)
    xn = jnp.concatenate([xn_p, xn_s], axis=0)
    y_sorted = _experts(layer, tile_expert, n_tiles_used, slot_pair // 2, xn, w['w_gate'], w['w_up'], w['w_down'])
    slots = pair_slot.reshape(-1, 2)
    rows = lambda idx: jnp.take(y_sorted, idx, axis=0, mode='clip')
    out_p = _combine(hp, rows(slots[:tp, 0]), rows(slots[:tp, 1]), rec_p, g_final, final_norm=final_norm)
    out_s = _combine(hs, rows(slots[tp:, 0]), rows(slots[tp:, 1]), rec_s, g_final, final_norm=final_norm)
    return out_p, out_s


_TAIL_ROWS = 32
_TAIL_CTX = 64


def _tail_in_kernel(h_ref, gmix_ref, w_ref, xpool_ref, xconv_ref):
    u = _dot(_rms(h_ref[...], gmix_ref[...]), w_ref[...])
    xpool_ref[...] = u[:, :POOL_WIDTH]
    xconv_ref[...] = u[:, POOL_WIDTH:POOL_WIDTH + CONV_WIDTH] * jax.nn.sigmoid(u[:, POOL_WIDTH + CONV_WIDTH:])


def _tail_route(l, w, lw, h_in, o_mla, mem):
    t = h_in.shape[0]
    h_ctx = h_in[t - _TAIL_CTX:]
    w_pool_glu = w['w_in'][l][:, Q_LORA + KV_LORA + ROPE_DIM:]
    shp = jax.ShapeDtypeStruct((_TAIL_CTX, POOL_WIDTH), F32)
    xpool, xconv = pl.pallas_call(_tail_in_kernel, out_shape=[shp, shp], compiler_params=_params(None))(
        h_ctx, lw['g_mix'], w_pool_glu)
    pooled, cact = _poolconv_prompt(xpool, xconv, lw, pos0=t - _TAIL_CTX, out_dtype=F32)
    lw32 = dict(lw, w_pool=w['w_pool'][l], w_conv_pw=w['w_conv_pw'][l], w_out=w['w_out'][l],
                w_mq=w['w_mq'][l] * MEM_SCALE, w_mk=w['w_mk'][l], w_mv=w['w_mv'][l], w_mo=w['w_mo'][l])
    keep = _TAIL_CTX - _TAIL_ROWS
    h1 = _mix_out(h_ctx[keep:], o_mla[t - _TAIL_ROWS:], pooled[keep:], cact[keep:], lw32)
    mk, mv = _mem_kv(mem, lw32)
    h2 = _cross_prompt(h1, mk, mv, lw32)
    return _route(h2, lw)[1]


def _rot_cols(w):
    half = ROPE_DIM // 2
    return jnp.concatenate([-w[..., half:], w[..., :half]], axis=-1)


def _rope_table(pos):
    half = ROPE_DIM // 2
    inv = ROPE_THETA ** (-jnp.arange(half, dtype=F32) / half)
    ang = pos.astype(F32)[:, None] * inv[None, :]
    cos, sin = jnp.cos(ang), jnp.sin(ang)
    return jnp.concatenate([cos, cos, sin, sin], axis=1)


def _layer_weights(l, w):
    o1, o2, o3, o4 = Q_LORA, Q_LORA + KV_LORA, Q_LORA + KV_LORA + ROPE_DIM, Q_LORA + KV_LORA + ROPE_DIM + POOL_WIDTH
    w_in = w['w_in'][l]
    w_kpe = w_in[:, o2:o3]
    w_in_r = jnp.concatenate([w_in[:, :o2], w_in[:, o3:], w_kpe, _rot_cols(w_kpe)], axis=1).astype(BF16)
    del o1, o4
    w_uq = w['w_uq'][l] * (MLA_SCALE * LOG2_E)
    w_pe = w_uq[..., NOPE_DIM:]
    w_q = jnp.concatenate([w_uq[..., :NOPE_DIM], w_pe, _rot_cols(w_pe)], axis=-1)
    w_q = w_q.reshape(Q_LORA, MLA_HEADS * Q_CHUNK).astype(BF16)
    w_uk = w['w_uk'][l]
    w_uv = w['w_uv'][l]
    w_route = jnp.zeros((D_MODEL, ROUTE_LANES), F32)
    w_route = w_route.at[:, :N_GROUPS].set(w['w_route_group'][l]).at[:, N_GROUPS:N_GROUPS + N_EXPERTS].set(w['w_route_expert'][l])
    b_route = jnp.zeros((1, ROUTE_LANES), F32)
    b_route = b_route.at[0, :N_GROUPS].set(w['b_route_group'][l]).at[0, N_GROUPS:N_GROUPS + N_EXPERTS].set(w['b_route_expert'][l])
    vec = lambda name: w[name][l].reshape(1, -1)
    return {
        'g_mix': vec('g_mix'), 'w_in': w_in_r, 'g_q_lat': vec('g_q_lat'), 'w_q': w_q, 'g_kv_lat': vec('g_kv_lat'),
        'w_uk': w_uk.reshape(KV_LORA, MLA_HEADS * NOPE_DIM).astype(BF16),
        'w_uv': w_uv.reshape(KV_LORA, MLA_HEADS * V_DIM).astype(BF16),
        'w_uk_t': jnp.transpose(w_uk, (1, 2, 0)).astype(BF16),
        'w_uv_h': jnp.transpose(w_uv, (1, 0, 2)).astype(BF16),
        'w_pool': w['w_pool'][l].astype(BF16), 'pool_scale': vec('pool_scale'),
        'w_conv_dw': w['w_conv_dw'][l], 'b_conv_dw': vec('b_conv_dw'),
        'g_conv_ln': vec('g_conv_ln'), 'b_conv_ln': vec('b_conv_ln'),
        'w_conv_pw': w['w_conv_pw'][l].astype(BF16), 'w_out': w['w_out'][l].astype(BF16),
        'g_mem_x': vec('g_mem_x'), 'g_mem_tok': vec('g_mem_tok'),
        'w_mq': (w['w_mq'][l] * MEM_SCALE).astype(BF16), 'w_mk': w['w_mk'][l].astype(BF16),
        'w_mv': w['w_mv'][l].astype(BF16), 'w_mo': w['w_mo'][l].astype(BF16),
        'g_ffn': vec('g_ffn'), 'w_route': w_route, 'b_route': b_route,
    }


def kernel(x_prompt, x_sample, mem_prompt, cache_ckv, cache_kpe, page_table, state_pool, state_conv, cache_mem_k, cache_mem_v, g_mix, w_in, g_q_lat, w_uq, g_kv_lat, w_uk, w_uv, w_pool, pool_scale, w_conv_dw, b_conv_dw, g_conv_ln, b_conv_ln, w_conv_pw, w_out, g_mem_x, g_mem_tok, w_mq, w_mk, w_mv, w_mo, g_ffn, w_route_group, b_route_group, w_route_expert, b_route_expert, w_gate, w_up, w_down, g_final):
    weights = dict(g_mix=g_mix, w_in=w_in, g_q_lat=g_q_lat, w_uq=w_uq, g_kv_lat=g_kv_lat, w_uk=w_uk, w_uv=w_uv,
                   w_pool=w_pool, pool_scale=pool_scale, w_conv_dw=w_conv_dw, b_conv_dw=b_conv_dw,
                   g_conv_ln=g_conv_ln, b_conv_ln=b_conv_ln, w_conv_pw=w_conv_pw, w_out=w_out, g_mem_x=g_mem_x,
                   g_mem_tok=g_mem_tok, w_mq=w_mq, w_mk=w_mk, w_mv=w_mv, w_mo=w_mo, g_ffn=g_ffn,
                   w_route_group=w_route_group, b_route_group=b_route_group, w_route_expert=w_route_expert,
                   b_route_expert=b_route_expert, w_gate=w_gate, w_up=w_up, w_down=w_down)
    n_prompt, t_prompt = x_prompt.shape[:2]
    n_dec, s_dec = x_sample.shape[:2]
    assert n_prompt == 1 and s_dec == 1
    tab_p = _rope_table(jnp.arange(t_prompt, dtype=jnp.int32))
    tab_s = _rope_table(jnp.full((n_dec,), PAST_LEN, jnp.int32))
    hp = x_prompt.reshape(t_prompt, D_MODEL)
    hs = x_sample.reshape(n_dec, D_MODEL)
    mem = mem_prompt.reshape(MEM_TOKENS, D_MODEL)
    g_fin = g_final.reshape(1, D_MODEL)
    cache_kpe_t = jnp.swapaxes(cache_kpe, 2, 3)
    mem_k_rows = cache_mem_k.reshape(DEPTH, n_dec, MEM_TOKENS * MEM_HEADS, MEM_HEAD_DIM)
    mem_v_rows = cache_mem_v.reshape(DEPTH, n_dec, MEM_TOKENS * MEM_HEADS, MEM_HEAD_DIM)
    outs = {k: [] for k in ('ckv_p', 'kpe_p', 'pool_p', 'conv_p', 'mk_p', 'mv_p', 'ckv_s', 'kpe_s', 'pool_s', 'conv_s')}
    for l in range(DEPTH):
        lw = _layer_weights(l, weights)
        last = l == DEPTH - 1

        q, k, v, ckv, kpe, xpool, xconv = _mixer_in(hp, tab_p, lw, prompt=True)
        o_mla = _flash_attention(q, k, v)
        pooled, cact = _poolconv_prompt(xpool, xconv, lw)
        rec_tail = None if last else _tail_route(l, weights, lw, hp, o_mla, mem)
        hp = _mix_out(hp, o_mla, pooled, cact, lw)
        outs['ckv_p'].append(ckv.reshape(1, t_prompt, KV_LORA))
        outs['kpe_p'].append(kpe.reshape(1, t_prompt, ROPE_DIM))
        outs['pool_p'].append(xpool[-POOL_STATE:].reshape(1, POOL_STATE, POOL_WIDTH))
        outs['conv_p'].append(xconv[-CONV_STATE:].reshape(1, CONV_STATE, CONV_WIDTH))
        mk, mv = _mem_kv(mem, lw)
        outs['mk_p'].append(mk.reshape(1, MEM_TOKENS, MEM_HEADS, MEM_HEAD_DIM))
        outs['mv_p'].append(mv.reshape(1, MEM_TOKENS, MEM_HEADS, MEM_HEAD_DIM))
        hp = _cross_prompt(hp, mk, mv, lw)

        qlat, qpe, ckv_s, kpe_s, xpool_s, xconv_s = _mixer_in(hs, tab_s, lw, prompt=False)
        o_lat = _paged_attention(l, page_table, jnp.transpose(qlat, (1, 0, 2)), jnp.transpose(qpe, (1, 0, 2)),
                                 ckv_s.reshape(n_dec, 1, KV_LORA), kpe_s.reshape(n_dec, 1, ROPE_DIM),
                                 cache_ckv, cache_kpe_t)
        o_mla_s = _value_up(jnp.transpose(o_lat, (1, 0, 2)), lw['w_uv_h'])
        pooled_s, cact_s = _poolconv_sample(jnp.transpose(state_pool[l], (1, 0, 2)),
                                            jnp.transpose(state_conv[l], (1, 0, 2)), xpool_s, xconv_s, lw)
        hs = _mix_out(hs, o_mla_s, pooled_s, cact_s, lw)
        outs['ckv_s'].append(ckv_s.reshape(n_dec, 1, KV_LORA))
        outs['kpe_s'].append(kpe_s.reshape(n_dec, 1, ROPE_DIM))
        outs['pool_s'].append(jnp.concatenate([state_pool[l][:, 1:], xpool_s[:, None, :]], axis=1))
        outs['conv_s'].append(jnp.concatenate([state_conv[l][:, 1:], xconv_s[:, None, :]], axis=1))
        hs = _cross_sample(l, hs, mem_k_rows, mem_v_rows, lw)

        hp, hs = _moe(l, hp, hs, lw, weights, g_fin, final_norm=last, rec_tail=rec_tail)

    stack = lambda name: jnp.stack(outs[name])
    return (hp.reshape(1, t_prompt, D_MODEL), hs.reshape(n_dec, 1, D_MODEL),
            stack('ckv_p'), stack('kpe_p'), stack('pool_p'), stack('conv_p'), stack('mk_p'), stack('mv_p'),
            stack('ckv_s'), stack('kpe_s'), stack('pool_s'), stack('conv_s'))
```
